```python
import jax, jax.numpy as jnp
from jax import lax
import numpy as np

D_MODEL = 1024
BATCH = 2
SEQ = 16384
DEPTH = 1
DEC_BATCH = 16
DEC_SEQ = 2048
PAST_LEN = 128

GRID_W = 64
NA_HEADS = 8
NA_HEAD_DIM = 64
NA_WIDTH = NA_HEADS * NA_HEAD_DIM
NA_WIN_ROWS = 8
NA_WIN_COLS = 16
GLA_HEADS = 4
GLA_DK = 64
GLA_DV = 128
GLA_KEY_WIDTH = GLA_HEADS * GLA_DK
GLA_WIDTH = GLA_HEADS * GLA_DV
GLA_GATE_RANK = 16
GLA_GATE_NORM = 16.0
GLA_CHUNK = 64
MIX_WIDTH = NA_WIDTH + GLA_WIDTH
IN_SIZES = (NA_WIDTH, NA_WIDTH, NA_WIDTH, GLA_KEY_WIDTH, GLA_KEY_WIDTH, GLA_WIDTH, GLA_WIDTH, 2 * GLA_GATE_RANK)
IN_WIDTH = sum(IN_SIZES)
MEM_TOKENS = 256
MEM_HEADS = 4
MEM_HEAD_DIM = D_MODEL // MEM_HEADS
N_EXPERTS = 16
EC_CAPACITY_FACTOR = 2
D_FF = 2 * D_MODEL
LN_EPS = 1e-5
RMS_EPS = 1e-5
DEEPNORM_ALPHA = (2 * DEPTH) ** 0.25
DEEPNORM_BETA = (8 * DEPTH) ** -0.25

kernel_name = 'hybrid_na_gla_ec_encoder'


def layer_norm(x, g, b):
    x32 = x.astype(jnp.float32)
    mu = jnp.mean(x32, axis=-1, keepdims=True)
    var = jnp.mean(jnp.square(x32 - mu), axis=-1, keepdims=True)
    y = (x32 - mu) * lax.rsqrt(var + LN_EPS) * g.astype(jnp.float32) + b.astype(jnp.float32)
    return y.astype(x.dtype)


def neighbourhood_attention(q, k, v, rpb):
    B, L, _ = q.shape
    rows = L // GRID_W
    kh = min(NA_WIN_ROWS, rows)
    kw = NA_WIN_COLS
    shp = (B, rows, GRID_W, NA_HEADS, NA_HEAD_DIM)
    qg = (q * NA_HEAD_DIM ** -0.5).reshape(shp)
    kg = k.reshape(shp)
    vg = v.reshape(shp)
    cols = np.arange(GRID_W)
    col_start = np.clip(cols - kw // 2, 0, GRID_W - kw)
    col_idx = col_start[:, None] + np.arange(kw)[None, :]
    col_off = col_idx - cols[:, None] + (NA_WIN_COLS - 1)
    bias_c = rpb[:, :, col_off]

    def row_block(r):
        rs = jnp.clip(r - kh // 2, 0, rows - kh)
        kb = lax.dynamic_slice_in_dim(kg, rs, kh, axis=1)[:, :, col_idx]
        vb = lax.dynamic_slice_in_dim(vg, rs, kh, axis=1)[:, :, col_idx]
        qr = lax.dynamic_index_in_dim(qg, r, axis=1, keepdims=False)
        row_off = rs + jnp.arange(kh) - r + (NA_WIN_ROWS - 1)
        bias = jnp.take(bias_c, row_off, axis=1).transpose(0, 2, 1, 3)
        s = jnp.einsum('bchd,bicjhd->bhcij', qr, kb).astype(jnp.float32) + bias[None].astype(jnp.float32)
        p = jax.nn.softmax(s.reshape(B, NA_HEADS, GRID_W, kh * kw), axis=-1)
        p = p.reshape(s.shape).astype(vb.dtype)
        return jnp.einsum('bhcij,bicjhd->bchd', p, vb)

    o = lax.map(row_block, jnp.arange(rows))
    return jnp.moveaxis(o, 0, 1).reshape(B, L, NA_WIDTH)


def gla_one_direction(q, k, v, log_a):
    B, L, H, dk = q.shape
    dv = v.shape[-1]
    n = L // GLA_CHUNK
    C = GLA_CHUNK
    q = q.reshape(B, n, C, H, dk)
    k = k.reshape(B, n, C, H, dk)
    v = v.reshape(B, n, C, H, dv)
    b = jnp.cumsum(log_a.reshape(B, n, C, H, dk), axis=2)
    b_last = b[:, :, -1:]
    qe = q * jnp.exp(b)
    ke = k * jnp.exp(-b)
    causal = jnp.tril(jnp.ones((C, C), dtype=bool))
    A = jnp.where(causal, jnp.einsum('bnihd,bnjhd->bnhij', qe, ke), 0.0)
    o_intra = jnp.einsum('bnhij,bnjhe->bnihe', A, v)
    kv = jnp.einsum('bnjhd,bnjhe->bnhde', k * jnp.exp(b_last - b), v)
    decay = jnp.exp(b_last[:, :, 0])

    def step(S, inp):
        dec, kvn = inp
        return dec[..., None] * S + kvn, S

    S0 = jnp.zeros((B, H, dk, dv), jnp.float32)
    _, S_prev = lax.scan(step, S0, (jnp.moveaxis(decay, 1, 0), jnp.moveaxis(kv, 1, 0)))
    S_prev = jnp.moveaxis(S_prev, 0, 1)
    o_inter = jnp.einsum('bnihd,bnhde->bnihe', qe, S_prev)
    return (o_intra + o_inter).reshape(B, L, H, dv)


def gla_mixer(q, k, v, r, gk_lr, gate_w2, gate_b, norm_g):
    B, L, _ = q.shape
    f32 = jnp.float32
    qh = q.astype(f32).reshape(B, L, GLA_HEADS, GLA_DK) * GLA_DK ** -0.5
    kh = k.astype(f32).reshape(B, L, GLA_HEADS, GLA_DK)
    vh = v.astype(f32).reshape(B, L, GLA_HEADS, GLA_DV)
    z = jnp.einsum('blsr,srk->blsk', gk_lr.astype(f32).reshape(B, L, 2, GLA_GATE_RANK), gate_w2.astype(f32))
    log_a = jax.nn.log_sigmoid(z + gate_b.astype(f32)) / GLA_GATE_NORM
    log_a = log_a.reshape(B, L, 2, GLA_HEADS, GLA_DK)
    o_fwd = gla_one_direction(qh, kh, vh, log_a[:, :, 0])
    flip = lambda t: jnp.flip(t, axis=1)
    o_bwd = flip(gla_one_direction(flip(qh), flip(kh), flip(vh), flip(log_a[:, :, 1])))
    o = o_fwd + o_bwd
    o = o * lax.rsqrt(jnp.mean(jnp.square(o), axis=-1, keepdims=True) + RMS_EPS) * norm_g.astype(f32)
    o = o.reshape(B, L, GLA_WIDTH) * jax.nn.silu(r.astype(f32))
    return o.astype(q.dtype)


def memory_cross_attention(h, mem, wq, wkv, wo):
    B, L, D = h.shape
    M = mem.shape[1]
    q = (h @ wq).reshape(B, L, MEM_HEADS, MEM_HEAD_DIM)
    kv = (mem @ wkv).reshape(B, M, 2, MEM_HEADS, MEM_HEAD_DIM)
    s = jnp.einsum('blhd,bmhd->bhlm', q, kv[:, :, 0]).astype(jnp.float32) * MEM_HEAD_DIM ** -0.5
    p = jax.nn.softmax(s, axis=-1).astype(h.dtype)
    o = jnp.einsum('bhlm,bmhd->blhd', p, kv[:, :, 1]).reshape(B, L, MEM_HEADS * MEM_HEAD_DIM)
    return o @ wo


def expert_choice_ffn(h, w_router, w_gate, w_up, w_down):
    B, L, D = h.shape
    n = B * L
    cap = EC_CAPACITY_FACTOR * n // N_EXPERTS
    hf = h.reshape(n, D)
    aff = jax.nn.softmax((hf @ w_router).astype(jnp.float32), axis=-1)
    gate, idx = lax.top_k(aff.T, cap)
    xe = hf[idx]
    a = jnp.einsum('ecd,edf->ecf', xe, w_gate)
    u = jnp.einsum('ecd,edf->ecf', xe, w_up)
    ye = jnp.einsum('ecf,efd->ecd', jax.nn.silu(a) * u, w_down) * gate[..., None].astype(h.dtype)
    y = jnp.zeros_like(hf).at[idx.reshape(-1)].add(ye.reshape(-1, D))
    return y.reshape(B, L, D)


def encoder_trunk(x, mem, ln_in_g, ln_in_b, w_in, na_rpb, gla_gate_w2, gla_gate_b, gla_norm_g, w_out,
                  ln1_g, ln1_b, mem_wq, mem_wkv, mem_wo, ln2_g, ln2_b,
                  w_router, w_gate, w_up, w_down, ln3_g, ln3_b):
    h = layer_norm(x, ln_in_g, ln_in_b)
    split_points = np.cumsum(IN_SIZES)[:-1].tolist()
    for l in range(DEPTH):
        q_na, k_na, v_na, q_g, k_g, v_g, r_g, gk = jnp.split(h @ w_in[l], split_points, axis=-1)
        na = neighbourhood_attention(q_na, k_na, v_na, na_rpb[l])
        gl = gla_mixer(q_g, k_g, v_g, r_g, gk, gla_gate_w2[l], gla_gate_b[l], gla_norm_g[l])
        mixed = jnp.concatenate([na, gl], axis=-1) @ w_out[l]
        h = layer_norm(DEEPNORM_ALPHA * h + mixed, ln1_g[l], ln1_b[l])
        h = layer_norm(DEEPNORM_ALPHA * h + memory_cross_attention(h, mem, mem_wq[l], mem_wkv[l], mem_wo[l]),
                       ln2_g[l], ln2_b[l])
        h = layer_norm(DEEPNORM_ALPHA * h + expert_choice_ffn(h, w_router[l], w_gate[l], w_up[l], w_down[l]),
                       ln3_g[l], ln3_b[l])
    return h


def setup_inputs(seed: int = 0) -> dict:
    key = jax.random.key(seed)
    ks = jax.random.split(key, 32)
    f32 = jnp.float32
    nrm = lambda k, shape, scale: jax.random.normal(k, shape, f32) * scale
    gain = lambda k, shape: 1.0 + 0.02 * jax.random.normal(k, shape, f32)
    beta = DEEPNORM_BETA
    col_scale = np.concatenate([
        np.full(NA_WIDTH, 1.0), np.full(NA_WIDTH, 1.0), np.full(NA_WIDTH, beta),
        np.full(GLA_KEY_WIDTH, 1.0), np.full(GLA_KEY_WIDTH, 1.0), np.full(GLA_WIDTH, beta),
        np.full(GLA_WIDTH, 1.0), np.full(2 * GLA_GATE_RANK, 1.0)]).astype(np.float32)
    w_in = nrm(ks[8], (DEPTH, D_MODEL, IN_WIDTH), D_MODEL ** -0.5) * jnp.asarray(col_scale)
    mem_wkv = nrm(ks[14], (DEPTH, D_MODEL, 2 * D_MODEL), D_MODEL ** -0.5) * jnp.asarray(
        np.concatenate([np.full(D_MODEL, 1.0), np.full(D_MODEL, beta)]).astype(np.float32))
    return {
        'x_prompt': nrm(ks[0], (BATCH, SEQ, D_MODEL), 1.0),
        'x_sample': nrm(ks[1], (DEC_BATCH, DEC_SEQ, D_MODEL), 1.0),
        'mem_prompt': nrm(ks[2], (BATCH, MEM_TOKENS, D_MODEL), 1.0),
        'mem_sample': nrm(ks[3], (DEC_BATCH, MEM_TOKENS, D_MODEL), 1.0),
        'ln_in_g': gain(ks[4], (D_MODEL,)),
        'ln_in_b': nrm(ks[5], (D_MODEL,), 0.02),
        'w_in': w_in,
        'na_rpb': nrm(ks[9], (DEPTH, NA_HEADS, 2 * NA_WIN_ROWS - 1, 2 * NA_WIN_COLS - 1), 0.02),
        'gla_gate_w2': nrm(ks[10], (DEPTH, 2, GLA_GATE_RANK, GLA_KEY_WIDTH), GLA_GATE_RANK ** -0.5),
        'gla_gate_b': nrm(ks[11], (DEPTH, 2, GLA_KEY_WIDTH), 0.1),
        'gla_norm_g': gain(ks[12], (DEPTH, GLA_DV)),
        'w_out': nrm(ks[13], (DEPTH, MIX_WIDTH, D_MODEL), MIX_WIDTH ** -0.5 * beta),
        'ln1_g': gain(ks[15], (DEPTH, D_MODEL)),
        'ln1_b': nrm(ks[16], (DEPTH, D_MODEL), 0.02),
        'mem_wq': nrm(ks[17], (DEPTH, D_MODEL, D_MODEL), D_MODEL ** -0.5),
        'mem_wkv': mem_wkv,
        'mem_wo': nrm(ks[18], (DEPTH, D_MODEL, D_MODEL), D_MODEL ** -0.5 * beta),
        'ln2_g': gain(ks[19], (DEPTH, D_MODEL)),
        'ln2_b': nrm(ks[20], (DEPTH, D_MODEL), 0.02),
        'w_router': nrm(ks[21], (DEPTH, D_MODEL, N_EXPERTS), D_MODEL ** -0.5),
        'w_gate': nrm(ks[22], (DEPTH, N_EXPERTS, D_MODEL, D_FF), D_MODEL ** -0.5),
        'w_up': nrm(ks[23], (DEPTH, N_EXPERTS, D_MODEL, D_FF), D_MODEL ** -0.5 * beta),
        'w_down': nrm(ks[24], (DEPTH, N_EXPERTS, D_FF, D_MODEL), D_FF ** -0.5 * beta),
        'ln3_g': gain(ks[25], (DEPTH, D_MODEL)),
        'ln3_b': nrm(ks[26], (DEPTH, D_MODEL), 0.02),
    }


def reference(x_prompt, x_sample, mem_prompt, mem_sample, ln_in_g, ln_in_b, w_in, na_rpb, gla_gate_w2,
              gla_gate_b, gla_norm_g, w_out, ln1_g, ln1_b, mem_wq, mem_wkv, mem_wo, ln2_g, ln2_b,
              w_router, w_gate, w_up, w_down, ln3_g, ln3_b):
    y_prompt = encoder_trunk(x_prompt, mem_prompt, ln_in_g, ln_in_b, w_in, na_rpb, gla_gate_w2, gla_gate_b,
                             gla_norm_g, w_out, ln1_g, ln1_b, mem_wq, mem_wkv, mem_wo, ln2_g, ln2_b,
                             w_router, w_gate, w_up, w_down, ln3_g, ln3_b)
    y_sample = encoder_trunk(x_sample, mem_sample, ln_in_g, ln_in_b, w_in, na_rpb, gla_gate_w2, gla_gate_b,
                             gla_norm_g, w_out, ln1_g, ln1_b, mem_wq, mem_wkv, mem_wo, ln2_g, ln2_b,
                             w_router, w_gate, w_up, w_down, ln3_g, ln3_b)
    return (y_prompt, y_sample)
```

```python
import functools

import numpy as np
import jax
import jax.numpy as jnp
from jax import lax
from jax.experimental import pallas as pl
from jax.experimental.pallas import tpu as pltpu

F32 = jnp.float32
BF16 = jnp.bfloat16
I32 = jnp.int32

D_MODEL = 1024
GRID_W = 64
NA_HEADS = 8
NA_HEAD_DIM = 64
NA_WIDTH = NA_HEADS * NA_HEAD_DIM
NA_WIN_ROWS = 8
NA_WIN_COLS = 16
GLA_HEADS = 4
GLA_DK = 64
GLA_DV = 128
GLA_KEY_WIDTH = GLA_HEADS * GLA_DK
GLA_WIDTH = GLA_HEADS * GLA_DV
GLA_GATE_RANK = 16
GLA_GATE_NORM = 16.0
GLA_CHUNK = 64
MEM_TOKENS = 256
MEM_HEADS = 4
MEM_HEAD_DIM = D_MODEL // MEM_HEADS
N_EXPERTS = 16
EC_CAPACITY_FACTOR = 2
D_FF = 2 * D_MODEL
LN_EPS = 1e-5
RMS_EPS = 1e-5
DEPTH = 1
DEEPNORM_ALPHA = (2 * DEPTH) ** 0.25

LANES = 128
NEG = -1e30
VMEM_LIMIT = 56 * 1024 * 1024

NA_QROWS = 8
NA_SLAB = 16
GLA_SUPER = 512
TOK_TILE = 512
FFN_ROWS = 256


def _dot(a, b):
    return jnp.dot(a, b, preferred_element_type=F32)


def _dot_nt(a, b):
    return lax.dot_general(a, b, (((1,), (1,)), ((), ())), preferred_element_type=F32)


def _dot_tn(a, b):
    return lax.dot_general(a, b, (((0,), (0,)), ((), ())), preferred_element_type=F32)


def _layer_norm(x, g, b):
    mu = jnp.mean(x, axis=-1, keepdims=True)
    xc = x - mu
    var = jnp.mean(xc * xc, axis=-1, keepdims=True)
    return xc * lax.rsqrt(var + LN_EPS) * g + b


def _split3(x):
    hi = x.astype(BF16)
    r1 = x - hi.astype(F32)
    mid = r1.astype(BF16)
    lo = (r1 - mid.astype(F32)).astype(BF16)
    return hi, mid, lo


def _params(*sem):
    return pltpu.CompilerParams(dimension_semantics=sem, vmem_limit_bytes=VMEM_LIMIT)


def _const_spec(shape):
    nd = len(shape)
    return pl.BlockSpec(shape, lambda *_: (0,) * nd)


def _in_proj_kernel(x_ref, g_ref, b_ref, wna_ref, wqk_ref, wv_ref, wr_ref, wgk_ref, w2_ref, gb_ref,
                    h0_ref, qkv_ref, qk_ref, v_ref, r_ref, la_ref):
    h0 = _layer_norm(x_ref[...], g_ref[...], b_ref[...])
    h0_ref[...] = h0
    hb = h0.astype(BF16)
    qkv_ref[...] = _dot(hb, wna_ref[...]).astype(BF16)
    qk_ref[...] = _dot(hb, wqk_ref[...])
    v_ref[...] = _dot(hb, wv_ref[...]).astype(BF16)
    r_ref[...] = _dot(hb, wr_ref[...])
    gk = _dot(hb, wgk_ref[...])
    z = _dot(gk.astype(BF16), w2_ref[...]) + gb_ref[...]
    log_sig = jnp.minimum(z, 0.0) - jnp.log1p(jnp.exp(-jnp.abs(z)))
    la_ref[...] = log_sig * (1.0 / GLA_GATE_NORM)


def _in_proj(x, g, b, wna, wqk, wv, wr, wgk, w2, gb):
    n = x.shape[0]
    tm = TOK_TILE
    row = lambda w: pl.BlockSpec((tm, w), lambda i: (i, 0))
    outs = (
        jax.ShapeDtypeStruct((n, D_MODEL), F32),
        jax.ShapeDtypeStruct((n, 3 * NA_WIDTH), BF16),
        jax.ShapeDtypeStruct((n, 2 * GLA_KEY_WIDTH), F32),
        jax.ShapeDtypeStruct((n, GLA_WIDTH), BF16),
        jax.ShapeDtypeStruct((n, GLA_WIDTH), F32),
        jax.ShapeDtypeStruct((n, 2 * GLA_KEY_WIDTH), F32),
    )
    return pl.pallas_call(
        _in_proj_kernel,
        grid=(n // tm,),
        in_specs=[row(D_MODEL), _const_spec(g.shape), _const_spec(b.shape), _const_spec(wna.shape),
                  _const_spec(wqk.shape), _const_spec(wv.shape), _const_spec(wr.shape),
                  _const_spec(wgk.shape), _const_spec(w2.shape), _const_spec(gb.shape)],
        out_specs=(row(D_MODEL), row(3 * NA_WIDTH), row(2 * GLA_KEY_WIDTH), row(GLA_WIDTH), row(GLA_WIDTH),
                   row(2 * GLA_KEY_WIDTH)),
        out_shape=outs,
        compiler_params=_params("parallel"),
    )(x, g, b, wna, wqk, wv, wr, wgk, w2, gb)


def _na_tables(rpb):
    kw = NA_WIN_COLS
    cols = np.arange(GRID_W)
    col_start = np.clip(cols - kw // 2, 0, GRID_W - kw)
    j = np.arange(GRID_W)
    col_ok = (j[None, :] >= col_start[:, None]) & (j[None, :] < col_start[:, None] + kw)
    col_off = np.clip(j[None, :] - cols[:, None] + (NA_WIN_COLS - 1), 0, 2 * NA_WIN_COLS - 2)
    nblk = NA_SLAB + NA_QROWS
    ro = np.arange(nblk) - 5
    ro_ok = (ro >= 0) & (ro < 2 * NA_WIN_ROWS - 1)
    ro_c = np.clip(ro, 0, 2 * NA_WIN_ROWS - 2)
    t = rpb[:, ro_c][:, :, col_off]
    ok = jnp.asarray(ro_ok[None, :, None, None] & col_ok[None, None])
    t = jnp.where(ok, t, NEG).astype(F32)
    flat = jnp.transpose(t, (0, 2, 1, 3)).reshape(NA_HEADS, GRID_W, nblk * GRID_W)
    shifted = jnp.concatenate([flat[:, :, GRID_W:], jnp.full((NA_HEADS, GRID_W, GRID_W), NEG, F32)], axis=-1)
    tab = jnp.stack([flat, shifted])

    a = np.arange(NA_QROWS)
    i = np.arange(NA_SLAB)
    masks = []
    for w in (np.maximum(a, 4), a, np.minimum(a, 4)):
        valid = (i[None, :] >= w[:, None]) & (i[None, :] < w[:, None] + NA_WIN_ROWS)
        m = np.where(valid, 0.0, NEG).astype(np.float32)
        masks.append(np.repeat(np.repeat(m, GRID_W, axis=0), GRID_W, axis=1))
    return tab, jnp.asarray(np.stack(masks))


def _na_kernel(q_ref, kp_ref, kc_ref, kn_ref, vp_ref, vc_ref, vn_ref, tab_ref, rm_ref, o_ref):
    half = NA_QROWS * GRID_W // 2
    k = jnp.concatenate([kp_ref[half:, :], kc_ref[...], kn_ref[:half, :]], axis=0)
    v = jnp.concatenate([vp_ref[half:, :], vc_ref[...], vn_ref[:half, :]], axis=0)
    rm = rm_ref[0]
    nq = NA_QROWS * GRID_W
    nk = NA_SLAB * GRID_W
    lane = lax.broadcasted_iota(I32, (nq, LANES), 1)
    first = lane < NA_HEAD_DIM
    for hp in range(NA_HEADS // 2):
        sl = slice(hp * LANES, (hp + 1) * LANES)
        q2 = q_ref[:, sl]
        k2 = k[:, sl]
        v2 = v[:, sl]
        pv = []
        for sub in range(2):
            h = 2 * hp + sub
            qm = jnp.where(first if sub == 0 else jnp.logical_not(first), q2, jnp.zeros_like(q2))
            s = _dot_nt(qm, k2) * (NA_HEAD_DIM ** -0.5)
            strips = []
            for a in range(NA_QROWS):
                par = a % 2
                off = ((8 - a) if par == 0 else (7 - a)) * GRID_W
                strips.append(s[a * GRID_W:(a + 1) * GRID_W, :] + tab_ref[par, h, :, off:off + nk])
            s = jnp.concatenate(strips, axis=0) + rm
            m = jnp.max(s, axis=-1, keepdims=True)
            e = jnp.exp(s - m)
            l = jnp.sum(e, axis=-1, keepdims=True)
            pv.append(_dot(e.astype(BF16), v2) / l)
        o_ref[:, sl] = jnp.where(first, pv[0], pv[1]).astype(BF16)


def _na(qkv, tab, rmask, batch, seq):
    n = batch * seq
    nq = NA_QROWS * GRID_W
    nblk = seq // nq
    assert seq % nq == 0 and nblk >= 2
    cur = lambda c: pl.BlockSpec((nq, NA_WIDTH), lambda b, r: (b * nblk + r, c))
    prv = lambda c: pl.BlockSpec((nq, NA_WIDTH), lambda b, r: (b * nblk + jnp.maximum(r - 1, 0), c))
    nxt = lambda c: pl.BlockSpec((nq, NA_WIDTH), lambda b, r: (b * nblk + jnp.minimum(r + 1, nblk - 1), c))
    variant = lambda b, r: (jnp.where(r == 0, 0, jnp.where(r == nblk - 1, 2, 1)), 0, 0)
    return pl.pallas_call(
        _na_kernel,
        grid=(batch, nblk),
        in_specs=[cur(0), prv(1), cur(1), nxt(1), prv(2), cur(2), nxt(2),
                  _const_spec(tab.shape), pl.BlockSpec((1,) + rmask.shape[1:], variant)],
        out_specs=pl.BlockSpec((nq, NA_WIDTH), lambda b, r: (b * nblk + r, 0)),
        out_shape=jax.ShapeDtypeStruct((n, NA_WIDTH), BF16),
        compiler_params=_params("parallel", "parallel"),
    )(qkv, qkv, qkv, qkv, qkv, qkv, qkv, tab, rmask)


def _gla_kernel(qkf_ref, vf_ref, laf_ref, qkb_ref, vb_ref, lab_ref, of_ref, ob_ref, sf_ref, sb_ref):
    c = GLA_CHUNK
    kwid = GLA_KEY_WIDTH
    nchunk = GLA_SUPER // c

    @pl.when(pl.program_id(1) == 0)
    def _():
        sf_ref[...] = jnp.zeros_like(sf_ref)
        sb_ref[...] = jnp.zeros_like(sb_ref)

    ri = lax.broadcasted_iota(I32, (c, c), 0)
    ci = lax.broadcasted_iota(I32, (c, c), 1)
    tri_f = (ci <= ri).astype(BF16)
    tri_b = (ci >= ri).astype(BF16)
    ar = lax.broadcasted_iota(I32, (c, kwid), 0)
    ac = lax.broadcasted_iota(I32, (c, kwid), 1) % c
    keep_f = ac <= ar
    keep_b = ac >= ar
    wr_ = lax.broadcasted_iota(I32, (kwid, kwid), 0) // c
    wc_ = lax.broadcasted_iota(I32, (kwid, kwid), 1) // GLA_DK
    wmask = wr_ == wc_
    vr_ = lax.broadcasted_iota(I32, (kwid, GLA_WIDTH), 0) // c
    vc_ = lax.broadcasted_iota(I32, (kwid, GLA_WIDTH), 1) // GLA_DV
    vmask = vr_ == vc_
    sr_ = lax.broadcasted_iota(I32, (GLA_WIDTH, kwid), 0) // GLA_DV
    sc_ = lax.broadcasted_iota(I32, (GLA_WIDTH, kwid), 1) // GLA_DK
    smask = sr_ == sc_

    def unit(qk, v, la, tri, keep, last, st_ref):
        q = qk[:, :kwid] * (GLA_DK ** -0.5)
        k = qk[:, kwid:]
        hi, mid, lo = _split3(la)
        b = _dot(tri, hi) + _dot(tri, mid) + _dot(tri, lo)
        bl = b[last:last + 1, :]
        qe = (q * jnp.exp(b)).astype(BF16)
        ke = (k * jnp.exp(-b)).astype(BF16)
        kd = (k * jnp.exp(bl - b)).astype(BF16)
        dec = jnp.exp(bl)
        wt = jnp.where(wmask, jnp.concatenate([ke] * GLA_HEADS, axis=0), jnp.zeros((kwid, kwid), BF16))
        a = jnp.where(keep, _dot_nt(qe, wt), 0.0)
        vbd = jnp.where(vmask, jnp.concatenate([v] * GLA_HEADS, axis=0), jnp.zeros((kwid, GLA_WIDTH), BF16))
        st = st_ref[...]
        o = _dot(a.astype(BF16), vbd) + _dot_nt(qe, st.astype(BF16))
        upd = _dot_tn(v, kd)
        st_ref[...] = st * dec + jnp.where(smask, upd, 0.0)
        return o

    def body(ic, carry):
        rf = pl.multiple_of(ic * c, c)
        rb = pl.multiple_of((nchunk - 1 - ic) * c, c)
        of_ref[pl.ds(rf, c), :] = unit(qkf_ref[pl.ds(rf, c), :], vf_ref[pl.ds(rf, c), :],
                                       laf_ref[pl.ds(rf, c), :], tri_f, keep_f, c - 1, sf_ref)
        ob_ref[pl.ds(rb, c), :] = unit(qkb_ref[pl.ds(rb, c), :], vb_ref[pl.ds(rb, c), :],
                                       lab_ref[pl.ds(rb, c), :], tri_b, keep_b, 0, sb_ref)
        return carry

    lax.fori_loop(0, nchunk, body, 0)


def _gla(qk, v, la, batch, seq):
    n = batch * seq
    ns = seq // GLA_SUPER
    assert seq % GLA_SUPER == 0
    fwd = lambda w, c: pl.BlockSpec((GLA_SUPER, w), lambda b, s: (b * ns + s, c))
    bwd = lambda w, c: pl.BlockSpec((GLA_SUPER, w), lambda b, s: (b * ns + ns - 1 - s, c))
    kw2 = 2 * GLA_KEY_WIDTH
    return pl.pallas_call(
        _gla_kernel,
        grid=(batch, ns),
        in_specs=[fwd(kw2, 0), fwd(GLA_WIDTH, 0), fwd(GLA_KEY_WIDTH, 0),
                  bwd(kw2, 0), bwd(GLA_WIDTH, 0), bwd(GLA_KEY_WIDTH, 1)],
        out_specs=(fwd(GLA_WIDTH, 0), bwd(GLA_WIDTH, 0)),
        out_shape=(jax.ShapeDtypeStruct((n, GLA_WIDTH), F32), jax.ShapeDtypeStruct((n, GLA_WIDTH), F32)),
        scratch_shapes=[pltpu.VMEM((GLA_WIDTH, GLA_KEY_WIDTH), F32), pltpu.VMEM((GLA_WIDTH, GLA_KEY_WIDTH), F32)],
        compiler_params=_params("parallel", "arbitrary"),
    )(qk, v, la, qk, v, la)


def _mix_out_kernel(na_ref, of_ref, ob_ref, r_ref, ng_ref, h0_ref, w1_ref, w2_ref, g_ref, b_ref, h1_ref):
    o = of_ref[...] + ob_ref[...]
    parts = []
    for h in range(GLA_HEADS):
        oh = o[:, h * GLA_DV:(h + 1) * GLA_DV]
        ms = jnp.mean(oh * oh, axis=-1, keepdims=True)
        parts.append(oh * lax.rsqrt(ms + RMS_EPS) * ng_ref[...])
    r = r_ref[...]
    gl = jnp.concatenate(parts, axis=-1) * (r * jax.nn.sigmoid(r))
    mixed = _dot(na_ref[...], w1_ref[...]) + _dot(gl.astype(BF16), w2_ref[...])
    h1_ref[...] = _layer_norm(DEEPNORM_ALPHA * h0_ref[...] + mixed, g_ref[...], b_ref[...])


def _mix_out(na, of, ob, r, ng, h0, w1, w2, g, b):
    n = na.shape[0]
    tm = TOK_TILE
    row = lambda w: pl.BlockSpec((tm, w), lambda i: (i, 0))
    return pl.pallas_call(
        _mix_out_kernel,
        grid=(n // tm,),
        in_specs=[row(NA_WIDTH), row(GLA_WIDTH), row(GLA_WIDTH), row(GLA_WIDTH), _const_spec(ng.shape),
                  row(D_MODEL), _const_spec(w1.shape), _const_spec(w2.shape), _const_spec(g.shape),
                  _const_spec(b.shape)],
        out_specs=row(D_MODEL),
        out_shape=jax.ShapeDtypeStruct((n, D_MODEL), F32),
        compiler_params=_params("parallel"),
    )(na, of, ob, r, ng, h0, w1, w2, g, b)


def _mem_kv_kernel(m_ref, w_ref, o_ref):
    o_ref[...] = _dot(m_ref[...].astype(BF16), w_ref[...]).astype(BF16)


def _mem_kv(mem, wkv):
    rows = mem.shape[0]
    tn = D_MODEL
    return pl.pallas_call(
        _mem_kv_kernel,
        grid=(rows // MEM_TOKENS, wkv.shape[1] // tn),
        in_specs=[pl.BlockSpec((MEM_TOKENS, D_MODEL), lambda i, j: (i, 0)),
                  pl.BlockSpec((D_MODEL, tn), lambda i, j: (0, j))],
        out_specs=pl.BlockSpec((MEM_TOKENS, tn), lambda i, j: (i, j)),
        out_shape=jax.ShapeDtypeStruct((rows, wkv.shape[1]), BF16),
        compiler_params=_params("parallel", "parallel"),
    )(mem, wkv)


def _xattn_kernel(h1_ref, kv_ref, wq_ref, wo_ref, g_ref, b_ref, wrh_ref, wrl_ref, h2_ref, h2a_ref, aff_ref):
    h1 = h1_ref[...]
    q = _dot(h1.astype(BF16), wq_ref[...]).astype(BF16)
    outs = []
    for h in range(MEM_HEADS):
        sl = slice(h * MEM_HEAD_DIM, (h + 1) * MEM_HEAD_DIM)
        kh = kv_ref[:, sl]
        vh = kv_ref[:, D_MODEL + h * MEM_HEAD_DIM:D_MODEL + (h + 1) * MEM_HEAD_DIM]
        s = _dot_nt(q[:, sl], kh) * (MEM_HEAD_DIM ** -0.5)
        m = jnp.max(s, axis=-1, keepdims=True)
        e = jnp.exp(s - m)
        p = (e / jnp.sum(e, axis=-1, keepdims=True)).astype(BF16)
        outs.append(_dot(p, vh))
    o = jnp.concatenate(outs, axis=-1).astype(BF16)
    h2 = _layer_norm(DEEPNORM_ALPHA * h1 + _dot(o, wo_ref[...]), g_ref[...], b_ref[...])
    h2_ref[...] = h2
    h2a_ref[...] = DEEPNORM_ALPHA * h2
    hh = h2.astype(BF16)
    hl = (h2 - hh.astype(F32)).astype(BF16)
    logits = _dot_nt(wrh_ref[...], hh) + _dot_nt(wrh_ref[...], hl) + _dot_nt(wrl_ref[...], hh)
    m = jnp.max(logits, axis=0, keepdims=True)
    e = jnp.exp(logits - m)
    aff_ref[...] = e / jnp.sum(e, axis=0, keepdims=True)


def _xattn(h1, kv, wq, wo, g, b, wrh, wrl, batch, seq):
    n = batch * seq
    tm = TOK_TILE
    nt = seq // tm
    row = pl.BlockSpec((tm, D_MODEL), lambda i: (i, 0))
    return pl.pallas_call(
        _xattn_kernel,
        grid=(n // tm,),
        in_specs=[row, pl.BlockSpec((MEM_TOKENS, 2 * D_MODEL), lambda i: (i // nt, 0)),
                  _const_spec(wq.shape), _const_spec(wo.shape), _const_spec(g.shape), _const_spec(b.shape),
                  _const_spec(wrh.shape), _const_spec(wrl.shape)],
        out_specs=(row, row, pl.BlockSpec((N_EXPERTS, tm), lambda i: (0, i))),
        out_shape=(jax.ShapeDtypeStruct((n, D_MODEL), F32), jax.ShapeDtypeStruct((n, D_MODEL), F32),
                   jax.ShapeDtypeStruct((N_EXPERTS, n), F32)),
        compiler_params=_params("parallel"),
    )(h1, kv, wq, wo, g, b, wrh, wrl)


def _route_kernel(aff_ref, idx_ref, gate_ref, *, cap, pchunk):
    a = aff_ref[0]
    nb = a.shape[0]
    bits = lax.bitcast_convert_type(a, I32)

    def total(x):
        return jnp.sum(jnp.sum(x, axis=0, keepdims=True), axis=1, keepdims=True)

    def search(_, carry):
        lo, hi = carry
        mid = lo + ((hi - lo + 1) >> 1)
        ok = total((bits >= mid).astype(F32)) >= cap
        return jnp.where(ok, mid, lo), jnp.where(ok, hi, mid - 1)

    lo0 = jnp.zeros((1, 1), I32)
    hi0 = jnp.full((1, 1), 0x7F800000, I32)
    thr, _ = lax.fori_loop(0, 31, search, (lo0, hi0))

    li = lax.broadcasted_iota(I32, (LANES, LANES), 0)
    lj = lax.broadcasted_iota(I32, (LANES, LANES), 1)
    upper = (li <= lj).astype(BF16)
    bi = lax.broadcasted_iota(I32, (nb, nb), 0)
    bj = lax.broadcasted_iota(I32, (nb, nb), 1)
    strict = (bj < bi).astype(BF16)
    incl = (bi <= bj).astype(BF16)
    ones8 = jnp.ones((8, LANES), BF16)

    def prefix(mask_b):
        rowcum = _dot(mask_b, upper)
        rowtot = jnp.broadcast_to(rowcum[:, LANES - 1:LANES], (nb, LANES)).astype(BF16)
        return rowcum + _dot(strict, rowtot)

    gt = bits > thr
    eq = bits == thr
    need = cap - total(gt.astype(F32))
    sel = jnp.logical_or(gt, jnp.logical_and(eq, prefix(eq.astype(BF16)) <= need))
    selb = sel.astype(BF16)
    cnt = prefix(selb)

    rt_row = _dot_nt(ones8, selb)
    bend = _dot(rt_row.astype(BF16), incl)
    bstart = bend - rt_row

    cnt_hi = jnp.floor(cnt * (1.0 / 64.0))
    cnt_lo = cnt - 64.0 * cnt_hi
    a_hi, a_mid, a_lo = _split3(a)
    rowid = lax.broadcasted_iota(I32, (nb, LANES), 0).astype(BF16)
    rhs = jnp.concatenate([cnt_hi.astype(BF16), cnt_lo.astype(BF16), selb, rowid, a_hi, a_mid, a_lo], axis=1)
    lanef = lax.broadcasted_iota(I32, (pchunk, LANES), 1).astype(F32)

    for pc in range(cap // pchunk):
        p = (lax.broadcasted_iota(I32, (pchunk, nb), 0) + pc * pchunk).astype(F32)
        g = jnp.logical_and(bstart[0:1, :] <= p, p < bend[0:1, :]).astype(BF16)
        rows = _dot(g, rhs)
        part = lambda k: rows[:, k * LANES:(k + 1) * LANES]
        crow = part(0) * 64.0 + part(1)
        p1 = (lax.broadcasted_iota(I32, (pchunk, LANES), 0) + (pc * pchunk + 1)).astype(F32)
        oh = jnp.logical_and(crow == p1, part(2) > 0.5)
        pick = lambda x: jnp.where(oh, x, 0.0)
        tok_row = _dot_nt(ones8, pick(part(3)).astype(BF16))
        tok_lane = _dot_nt(ones8, pick(lanef).astype(BF16))
        idx_ref[0, :, pc * pchunk:(pc + 1) * pchunk] = (tok_row[0:1] * float(LANES) + tok_lane[0:1]).astype(I32)
        gate = jnp.sum(pick(part(4) + part(5) + part(6)), axis=1, keepdims=True)
        gate_ref[pc * pchunk:(pc + 1) * pchunk, :] = jnp.broadcast_to(gate, (pchunk, LANES))


def _route(aff_t, cap):
    e, n = aff_t.shape
    nb = n // LANES
    assert nb <= 256 and cap % 64 == 0 and cap // 64 <= 256
    pchunk = min(1024, cap)
    aff3 = aff_t.reshape(e, nb, LANES)
    return pl.pallas_call(
        functools.partial(_route_kernel, cap=cap, pchunk=pchunk),
        grid=(e,),
        in_specs=[pl.BlockSpec((1, nb, LANES), lambda i: (i, 0, 0))],
        out_specs=(pl.BlockSpec((1, 1, cap), lambda i: (i, 0, 0)),
                   pl.BlockSpec((cap, LANES), lambda i: (i, 0))),
        out_shape=(jax.ShapeDtypeStruct((e, 1, cap), I32), jax.ShapeDtypeStruct((e * cap, LANES), F32)),
        compiler_params=_params("parallel"),
    )(aff3)


def _ffn_kernel(idx_ref, gate_ref, wg_ref, wu_ref, wd_ref, h2_hbm, acc_in_hbm, acc_hbm,
                xbuf, ybuf, semx, semy, semo):
    del acc_in_hbm

    def x_copy(i):
        return pltpu.make_async_copy(h2_hbm.at[pl.ds(idx_ref[0, 0, i], 1)], xbuf.at[pl.ds(i, 1)], semx)

    def y_copy(i):
        return pltpu.make_async_copy(acc_hbm.at[pl.ds(idx_ref[0, 0, i], 1)], ybuf.at[pl.ds(i, 1)], semy)

    def o_copy(i):
        return pltpu.make_async_copy(ybuf.at[pl.ds(i, 1)], acc_hbm.at[pl.ds(idx_ref[0, 0, i], 1)], semo)

    def each(fn):
        def body(i, carry):
            fn(i)
            return carry
        lax.fori_loop(0, FFN_ROWS, body, 0)

    each(lambda i: (x_copy(i).start(), y_copy(i).start()))
    each(lambda i: (x_copy(i).wait(), y_copy(i).wait()))
    x = xbuf[...].astype(BF16)
    a = _dot(x, wg_ref[0])
    u = _dot(x, wu_ref[0])
    hmid = (a * jax.nn.sigmoid(a) * u).astype(BF16)
    ye = _dot(hmid, wd_ref[0]) * gate_ref[:, 0:1]
    ybuf[...] = ybuf[...] + ye
    each(lambda i: o_copy(i).start())
    each(lambda i: o_copy(i).wait())


def _ffn(idx, gate, wg, wu, wd, h2, acc, cap):
    e = idx.shape[0]
    tr = FFN_ROWS
    per = cap // tr
    assert cap % tr == 0
    wspec = lambda shp: pl.BlockSpec((1,) + shp, lambda s: (s // per, 0, 0))
    return pl.pallas_call(
        _ffn_kernel,
        grid=(e * per,),
        in_specs=[pl.BlockSpec((1, 1, tr), lambda s: (s // per, 0, s % per), memory_space=pltpu.SMEM),
                  pl.BlockSpec((tr, LANES), lambda s: (s, 0)),
                  wspec((D_MODEL, D_FF)), wspec((D_MODEL, D_FF)), wspec((D_FF, D_MODEL)),
                  pl.BlockSpec(memory_space=pl.ANY), pl.BlockSpec(memory_space=pl.ANY)],
        out_specs=pl.BlockSpec(memory_space=pl.ANY),
        out_shape=jax.ShapeDtypeStruct(acc.shape, F32),
        scratch_shapes=[pltpu.VMEM((tr, D_MODEL), F32), pltpu.VMEM((tr, D_MODEL), F32),
                        pltpu.SemaphoreType.DMA, pltpu.SemaphoreType.DMA, pltpu.SemaphoreType.DMA],
        input_output_aliases={6: 0},
        compiler_params=_params("arbitrary"),
    )(idx, gate, wg, wu, wd, h2, acc)


def _ln_out_kernel(x_ref, g_ref, b_ref, o_ref):
    o_ref[...] = _layer_norm(x_ref[...], g_ref[...], b_ref[...])


def _ln_out(x, g, b):
    n = x.shape[0]
    tm = TOK_TILE
    row = pl.BlockSpec((tm, D_MODEL), lambda i: (i, 0))
    return pl.pallas_call(
        _ln_out_kernel,
        grid=(n // tm,),
        in_specs=[row, _const_spec(g.shape), _const_spec(b.shape)],
        out_specs=row,
        out_shape=jax.ShapeDtypeStruct((n, D_MODEL), F32),
        compiler_params=_params("parallel"),
    )(x, g, b)


def _prepare(ln_in_g, ln_in_b, w_in, na_rpb, gla_gate_w2, gla_gate_b, gla_norm_g, w_out, ln1_g, ln1_b,
             mem_wq, mem_wkv, mem_wo, ln2_g, ln2_b, w_router, w_gate, w_up, w_down, ln3_g, ln3_b):
    row = lambda v: v.reshape(1, -1).astype(F32)
    w = w_in[0]
    o = np.cumsum((0, NA_WIDTH, NA_WIDTH, NA_WIDTH, GLA_KEY_WIDTH, GLA_KEY_WIDTH, GLA_WIDTH, GLA_WIDTH,
                   2 * GLA_GATE_RANK))
    wgk = jnp.zeros((D_MODEL, LANES), F32).at[:, :2 * GLA_GATE_RANK].set(w[:, o[7]:o[8]])
    w2 = jnp.zeros((LANES, 2 * GLA_KEY_WIDTH), F32)
    for s in range(2):
        w2 = w2.at[s * GLA_GATE_RANK:(s + 1) * GLA_GATE_RANK,
                   s * GLA_KEY_WIDTH:(s + 1) * GLA_KEY_WIDTH].set(gla_gate_w2[0, s])
    tab, rmask = _na_tables(na_rpb[0])
    wr_t = w_router[0].T.astype(F32)
    wr_hi = wr_t.astype(BF16)
    wr_lo = (wr_t - wr_hi.astype(F32)).astype(BF16)
    return dict(
        ln_in=(row(ln_in_g), row(ln_in_b)),
        wna=w[:, o[0]:o[3]].astype(BF16), wqk=w[:, o[3]:o[5]].astype(BF16), wv=w[:, o[5]:o[6]].astype(BF16),
        wr=w[:, o[6]:o[7]].astype(BF16), wgk=wgk.astype(BF16), w2=w2.astype(BF16),
        gb=gla_gate_b[0].reshape(1, -1).astype(F32),
        tab=tab, rmask=rmask, ng=row(gla_norm_g[0]),
        wo1=w_out[0][:NA_WIDTH].astype(BF16), wo2=w_out[0][NA_WIDTH:].astype(BF16),
        ln1=(row(ln1_g[0]), row(ln1_b[0])),
        wq=mem_wq[0].astype(BF16), wkv=mem_wkv[0].astype(BF16), wmo=mem_wo[0].astype(BF16),
        ln2=(row(ln2_g[0]), row(ln2_b[0])),
        wr_hi=wr_hi, wr_lo=wr_lo,
        wg=w_gate[0].astype(BF16), wu=w_up[0].astype(BF16), wd=w_down[0].astype(BF16),
        ln3=(row(ln3_g[0]), row(ln3_b[0])),
    )


def _trunk(x, mem, p):
    batch, seq, _ = x.shape
    n = batch * seq
    cap = EC_CAPACITY_FACTOR * n // N_EXPERTS
    h0, qkv, qk, v, r, la = _in_proj(x.reshape(n, D_MODEL), *p["ln_in"], p["wna"], p["wqk"], p["wv"], p["wr"],
                                     p["wgk"], p["w2"], p["gb"])
    na = _na(qkv, p["tab"], p["rmask"], batch, seq)
    of, ob = _gla(qk, v, la, batch, seq)
    h1 = _mix_out(na, of, ob, r, p["ng"], h0, p["wo1"], p["wo2"], *p["ln1"])
    kv = _mem_kv(mem.reshape(batch * MEM_TOKENS, D_MODEL), p["wkv"])
    h2, acc, aff_t = _xattn(h1, kv, p["wq"], p["wmo"], *p["ln2"], p["wr_hi"], p["wr_lo"], batch, seq)
    idx, gate = _route(aff_t, cap)
    acc = _ffn(idx, gate, p["wg"], p["wu"], p["wd"], h2, acc, cap)
    return _ln_out(acc, *p["ln3"]).reshape(batch, seq, D_MODEL)


def kernel(x_prompt, x_sample, mem_prompt, mem_sample, ln_in_g, ln_in_b, w_in, na_rpb, gla_gate_w2, gla_gate_b,
           gla_norm_g, w_out, ln1_g, ln1_b, mem_wq, mem_wkv, mem_wo, ln2_g, ln2_b, w_router, w_gate, w_up, w_down,
           ln3_g, ln3_b):
    p = _prepare(ln_in_g, ln_in_b, w_in, na_rpb, gla_gate_w2, gla_gate_b, gla_norm_g, w_out, ln1_g, ln1_b,
                 mem_wq, mem_wkv, mem_wo, ln2_g, ln2_b, w_router, w_gate, w_up, w_down, ln3_g, ln3_b)
    return _trunk(x_prompt, mem_prompt, p), _trunk(x_sample, mem_sample, p)
```

```python
import functools

import numpy as np
import jax
import jax.numpy as jnp
from jax import lax
from jax.experimental import pallas as pl
from jax.experimental.pallas import tpu as pltpu

F32 = jnp.float32
BF16 = jnp.bfloat16
I32 = jnp.int32

D_MODEL = 1024
GRID_W = 64
NA_HEADS = 8
NA_HEAD_DIM = 64
NA_WIDTH = NA_HEADS * NA_HEAD_DIM
NA_WIN_ROWS = 8
NA_WIN_COLS = 16
GLA_HEADS = 4
GLA_DK = 64
GLA_DV = 128
GLA_KEY_WIDTH = GLA_HEADS * GLA_DK
GLA_WIDTH = GLA_HEADS * GLA_DV
GLA_GATE_RANK = 16
GLA_GATE_NORM = 16.0
GLA_CHUNK = 64
MEM_TOKENS = 256
MEM_HEADS = 4
MEM_HEAD_DIM = D_MODEL // MEM_HEADS
N_EXPERTS = 16
EC_CAPACITY_FACTOR = 2
D_FF = 2 * D_MODEL
LN_EPS = 1e-5
RMS_EPS = 1e-5
DEPTH = 1
DEEPNORM_ALPHA = (2 * DEPTH) ** 0.25

LANES = 128
SUBLANES = 8
NEG = -1e30
VMEM_LIMIT = 56 * 1024 * 1024

NA_QROWS = 8
NA_SLAB = 16
GLA_SUPER = 512
TOK_TILE = 512
FFN_ROWS = 256
CMB_TILE = 256
CMB_WIN = 64


def _dot(a, b):
    return jnp.dot(a, b, preferred_element_type=F32)


def _dot_nt(a, b):
    return lax.dot_general(a, b, (((1,), (1,)), ((), ())), preferred_element_type=F32)


def _dot_tn(a, b):
    return lax.dot_general(a, b, (((0,), (0,)), ((), ())), preferred_element_type=F32)


def _layer_norm(x, g, b):
    mu = jnp.mean(x, axis=-1, keepdims=True)
    xc = x - mu
    var = jnp.mean(xc * xc, axis=-1, keepdims=True)
    return xc * lax.rsqrt(var + LN_EPS) * g + b


def _split3(x):
    hi = x.astype(BF16)
    r1 = x - hi.astype(F32)
    mid = r1.astype(BF16)
    lo = (r1 - mid.astype(F32)).astype(BF16)
    return hi, mid, lo


def _params(*sem):
    return pltpu.CompilerParams(dimension_semantics=sem, vmem_limit_bytes=VMEM_LIMIT)


def _const_spec(shape):
    nd = len(shape)
    return pl.BlockSpec(shape, lambda *_: (0,) * nd)


def _in_proj_kernel(x_ref, g_ref, b_ref, wna_ref, wqk_ref, wv_ref, wr_ref, wgk_ref, w2_ref, gb_ref,
                    h0_ref, qkv_ref, qk_ref, v_ref, r_ref, la_ref):
    h0 = _layer_norm(x_ref[...], g_ref[...], b_ref[...])
    h0_ref[...] = h0
    hb = h0.astype(BF16)
    qkv_ref[...] = _dot(hb, wna_ref[...]).astype(BF16)
    qk_ref[...] = _dot(hb, wqk_ref[...])
    v_ref[...] = _dot(hb, wv_ref[...]).astype(BF16)
    r_ref[...] = _dot(hb, wr_ref[...])
    gk = _dot(hb, wgk_ref[...])
    z = _dot(gk.astype(BF16), w2_ref[...]) + gb_ref[...]
    log_sig = jnp.minimum(z, 0.0) - jnp.log1p(jnp.exp(-jnp.abs(z)))
    la_ref[...] = log_sig * (1.0 / GLA_GATE_NORM)


def _in_proj(x, g, b, wna, wqk, wv, wr, wgk, w2, gb):
    n = x.shape[0]
    tm = TOK_TILE
    row = lambda w: pl.BlockSpec((tm, w), lambda i: (i, 0))
    outs = (
        jax.ShapeDtypeStruct((n, D_MODEL), F32),
        jax.ShapeDtypeStruct((n, 3 * NA_WIDTH), BF16),
        jax.ShapeDtypeStruct((n, 2 * GLA_KEY_WIDTH), F32),
        jax.ShapeDtypeStruct((n, GLA_WIDTH), BF16),
        jax.ShapeDtypeStruct((n, GLA_WIDTH), F32),
        jax.ShapeDtypeStruct((n, 2 * GLA_KEY_WIDTH), F32),
    )
    return pl.pallas_call(
        _in_proj_kernel,
        grid=(n // tm,),
        in_specs=[row(D_MODEL), _const_spec(g.shape), _const_spec(b.shape), _const_spec(wna.shape),
                  _const_spec(wqk.shape), _const_spec(wv.shape), _const_spec(wr.shape),
                  _const_spec(wgk.shape), _const_spec(w2.shape), _const_spec(gb.shape)],
        out_specs=(row(D_MODEL), row(3 * NA_WIDTH), row(2 * GLA_KEY_WIDTH), row(GLA_WIDTH), row(GLA_WIDTH),
                   row(2 * GLA_KEY_WIDTH)),
        out_shape=outs,
        name="ln_in_proj",
        compiler_params=_params("parallel"),
    )(x, g, b, wna, wqk, wv, wr, wgk, w2, gb)


def _na_tables(rpb):
    kw = NA_WIN_COLS
    cols = np.arange(GRID_W)
    col_start = np.clip(cols - kw // 2, 0, GRID_W - kw)
    j = np.arange(GRID_W)
    col_ok = (j[None, :] >= col_start[:, None]) & (j[None, :] < col_start[:, None] + kw)
    col_off = np.clip(j[None, :] - cols[:, None] + (NA_WIN_COLS - 1), 0, 2 * NA_WIN_COLS - 2)
    nblk = NA_SLAB + NA_QROWS
    ro = np.arange(nblk) - 5
    ro_ok = (ro >= 0) & (ro < 2 * NA_WIN_ROWS - 1)
    ro_c = np.clip(ro, 0, 2 * NA_WIN_ROWS - 2)
    t = rpb[:, ro_c][:, :, col_off]
    ok = jnp.asarray(ro_ok[None, :, None, None] & col_ok[None, None])
    t = jnp.where(ok, t, NEG).astype(F32)
    flat = jnp.transpose(t, (0, 2, 1, 3)).reshape(NA_HEADS, GRID_W, nblk * GRID_W)
    shifted = jnp.concatenate([flat[:, :, GRID_W:], jnp.full((NA_HEADS, GRID_W, GRID_W), NEG, F32)], axis=-1)
    tab = jnp.stack([flat, shifted])

    a = np.arange(NA_QROWS)
    i = np.arange(NA_SLAB)
    masks = []
    for w in (np.maximum(a, 4), a, np.minimum(a, 4)):
        valid = (i[None, :] >= w[:, None]) & (i[None, :] < w[:, None] + NA_WIN_ROWS)
        m = np.where(valid, 0.0, NEG).astype(np.float32)
        masks.append(np.repeat(np.repeat(m, GRID_W, axis=0), GRID_W, axis=1))
    return tab, jnp.asarray(np.stack(masks))


def _na_kernel(q_ref, kp_ref, kc_ref, kn_ref, vp_ref, vc_ref, vn_ref, tab_ref, rm_ref, o_ref):
    half = NA_QROWS * GRID_W // 2
    k = jnp.concatenate([kp_ref[half:, :], kc_ref[...], kn_ref[:half, :]], axis=0)
    v = jnp.concatenate([vp_ref[half:, :], vc_ref[...], vn_ref[:half, :]], axis=0)
    rm = rm_ref[0]
    nq = NA_QROWS * GRID_W
    nk = NA_SLAB * GRID_W
    lane = lax.broadcasted_iota(I32, (nq, LANES), 1)
    first = lane < NA_HEAD_DIM
    for hp in range(NA_HEADS // 2):
        sl = slice(hp * LANES, (hp + 1) * LANES)
        q2 = q_ref[:, sl]
        k2 = k[:, sl]
        v2 = v[:, sl]
        pv = []
        for sub in range(2):
            h = 2 * hp + sub
            qm = jnp.where(first if sub == 0 else jnp.logical_not(first), q2, jnp.zeros_like(q2))
            s = _dot_nt(qm, k2) * (NA_HEAD_DIM ** -0.5)
            strips = []
            for a in range(NA_QROWS):
                par = a % 2
                off = ((8 - a) if par == 0 else (7 - a)) * GRID_W
                strips.append(s[a * GRID_W:(a + 1) * GRID_W, :] + tab_ref[par, h, :, off:off + nk])
            s = jnp.concatenate(strips, axis=0) + rm
            m = jnp.max(s, axis=-1, keepdims=True)
            e = jnp.exp(s - m)
            l = jnp.sum(e, axis=-1, keepdims=True)
            pv.append(_dot(e.astype(BF16), v2) / l)
        o_ref[:, sl] = jnp.where(first, pv[0], pv[1]).astype(BF16)


def _na(qkv, tab, rmask, batch, seq):
    n = batch * seq
    nq = NA_QROWS * GRID_W
    nblk = seq // nq
    assert seq % nq == 0 and nblk >= 2
    cur = lambda c: pl.BlockSpec((nq, NA_WIDTH), lambda b, r: (b * nblk + r, c))
    prv = lambda c: pl.BlockSpec((nq, NA_WIDTH), lambda b, r: (b * nblk + jnp.maximum(r - 1, 0), c))
    nxt = lambda c: pl.BlockSpec((nq, NA_WIDTH), lambda b, r: (b * nblk + jnp.minimum(r + 1, nblk - 1), c))
    variant = lambda b, r: (jnp.where(r == 0, 0, jnp.where(r == nblk - 1, 2, 1)), 0, 0)
    return pl.pallas_call(
        _na_kernel,
        grid=(batch, nblk),
        in_specs=[cur(0), prv(1), cur(1), nxt(1), prv(2), cur(2), nxt(2),
                  _const_spec(tab.shape), pl.BlockSpec((1,) + rmask.shape[1:], variant)],
        out_specs=pl.BlockSpec((nq, NA_WIDTH), lambda b, r: (b * nblk + r, 0)),
        out_shape=jax.ShapeDtypeStruct((n, NA_WIDTH), BF16),
        name="nbr_attn",
        compiler_params=_params("parallel", "parallel"),
    )(qkv, qkv, qkv, qkv, qkv, qkv, qkv, tab, rmask)


def _gla_kernel(qkf_ref, vf_ref, laf_ref, qkb_ref, vb_ref, lab_ref, of_ref, ob_ref, sf_ref, sb_ref):
    c = GLA_CHUNK
    kwid = GLA_KEY_WIDTH
    nchunk = GLA_SUPER // c

    @pl.when(pl.program_id(1) == 0)
    def _():
        sf_ref[...] = jnp.zeros_like(sf_ref)
        sb_ref[...] = jnp.zeros_like(sb_ref)

    ri = lax.broadcasted_iota(I32, (c, c), 0)
    ci = lax.broadcasted_iota(I32, (c, c), 1)
    tri_f = (ci <= ri).astype(BF16)
    tri_b = (ci >= ri).astype(BF16)
    ar = lax.broadcasted_iota(I32, (c, kwid), 0)
    ac = lax.broadcasted_iota(I32, (c, kwid), 1) % c
    keep_f = ac <= ar
    keep_b = ac >= ar
    wr_ = lax.broadcasted_iota(I32, (kwid, kwid), 0) // c
    wc_ = lax.broadcasted_iota(I32, (kwid, kwid), 1) // GLA_DK
    wmask = wr_ == wc_
    vr_ = lax.broadcasted_iota(I32, (kwid, GLA_WIDTH), 0) // c
    vc_ = lax.broadcasted_iota(I32, (kwid, GLA_WIDTH), 1) // GLA_DV
    vmask = vr_ == vc_
    sr_ = lax.broadcasted_iota(I32, (GLA_WIDTH, kwid), 0) // GLA_DV
    sc_ = lax.broadcasted_iota(I32, (GLA_WIDTH, kwid), 1) // GLA_DK
    smask = sr_ == sc_

    def unit(qk, v, la, tri, keep, last, st_ref):
        q = qk[:, :kwid] * (GLA_DK ** -0.5)
        k = qk[:, kwid:]
        hi, mid, lo = _split3(la)
        b = _dot(tri, hi) + _dot(tri, mid) + _dot(tri, lo)
        bl = b[last:last + 1, :]
        qe = (q * jnp.exp(b)).astype(BF16)
        ke = (k * jnp.exp(-b)).astype(BF16)
        kd = (k * jnp.exp(bl - b)).astype(BF16)
        dec = jnp.exp(bl)
        wt = jnp.where(wmask, jnp.concatenate([ke] * GLA_HEADS, axis=0), jnp.zeros((kwid, kwid), BF16))
        a = jnp.where(keep, _dot_nt(qe, wt), 0.0)
        vbd = jnp.where(vmask, jnp.concatenate([v] * GLA_HEADS, axis=0), jnp.zeros((kwid, GLA_WIDTH), BF16))
        st = st_ref[...]
        o = _dot(a.astype(BF16), vbd) + _dot_nt(qe, st.astype(BF16))
        upd = _dot_tn(v, kd)
        st_ref[...] = st * dec + jnp.where(smask, upd, 0.0)
        return o

    def body(ic, carry):
        rf = pl.multiple_of(ic * c, c)
        rb = pl.multiple_of((nchunk - 1 - ic) * c, c)
        of_ref[pl.ds(rf, c), :] = unit(qkf_ref[pl.ds(rf, c), :], vf_ref[pl.ds(rf, c), :],
                                       laf_ref[pl.ds(rf, c), :], tri_f, keep_f, c - 1, sf_ref)
        ob_ref[pl.ds(rb, c), :] = unit(qkb_ref[pl.ds(rb, c), :], vb_ref[pl.ds(rb, c), :],
                                       lab_ref[pl.ds(rb, c), :], tri_b, keep_b, 0, sb_ref)
        return carry

    lax.fori_loop(0, nchunk, body, 0)


def _gla(qk, v, la, batch, seq):
    n = batch * seq
    ns = seq // GLA_SUPER
    assert seq % GLA_SUPER == 0
    fwd = lambda w, c: pl.BlockSpec((GLA_SUPER, w), lambda b, s: (b * ns + s, c))
    bwd = lambda w, c: pl.BlockSpec((GLA_SUPER, w), lambda b, s: (b * ns + ns - 1 - s, c))
    kw2 = 2 * GLA_KEY_WIDTH
    return pl.pallas_call(
        _gla_kernel,
        grid=(batch, ns),
        in_specs=[fwd(kw2, 0), fwd(GLA_WIDTH, 0), fwd(GLA_KEY_WIDTH, 0),
                  bwd(kw2, 0), bwd(GLA_WIDTH, 0), bwd(GLA_KEY_WIDTH, 1)],
        out_specs=(fwd(GLA_WIDTH, 0), bwd(GLA_WIDTH, 0)),
        out_shape=(jax.ShapeDtypeStruct((n, GLA_WIDTH), F32), jax.ShapeDtypeStruct((n, GLA_WIDTH), F32)),
        scratch_shapes=[pltpu.VMEM((GLA_WIDTH, GLA_KEY_WIDTH), F32), pltpu.VMEM((GLA_WIDTH, GLA_KEY_WIDTH), F32)],
        name="gla",
        compiler_params=_params("parallel", "arbitrary"),
    )(qk, v, la, qk, v, la)


def _mix_out_kernel(na_ref, of_ref, ob_ref, r_ref, ng_ref, h0_ref, w1_ref, w2_ref, g_ref, b_ref, h1_ref):
    o = of_ref[...] + ob_ref[...]
    parts = []
    for h in range(GLA_HEADS):
        oh = o[:, h * GLA_DV:(h + 1) * GLA_DV]
        ms = jnp.mean(oh * oh, axis=-1, keepdims=True)
        parts.append(oh * lax.rsqrt(ms + RMS_EPS) * ng_ref[...])
    r = r_ref[...]
    gl = jnp.concatenate(parts, axis=-1) * (r * jax.nn.sigmoid(r))
    mixed = _dot(na_ref[...], w1_ref[...]) + _dot(gl.astype(BF16), w2_ref[...])
    h1_ref[...] = _layer_norm(DEEPNORM_ALPHA * h0_ref[...] + mixed, g_ref[...], b_ref[...])


def _mix_out(na, of, ob, r, ng, h0, w1, w2, g, b):
    n = na.shape[0]
    tm = TOK_TILE
    row = lambda w: pl.BlockSpec((tm, w), lambda i: (i, 0))
    return pl.pallas_call(
        _mix_out_kernel,
        grid=(n // tm,),
        in_specs=[row(NA_WIDTH), row(GLA_WIDTH), row(GLA_WIDTH), row(GLA_WIDTH), _const_spec(ng.shape),
                  row(D_MODEL), _const_spec(w1.shape), _const_spec(w2.shape), _const_spec(g.shape),
                  _const_spec(b.shape)],
        out_specs=row(D_MODEL),
        out_shape=jax.ShapeDtypeStruct((n, D_MODEL), F32),
        name="mix_out_ln1",
        compiler_params=_params("parallel"),
    )(na, of, ob, r, ng, h0, w1, w2, g, b)


def _mem_kv_kernel(m_ref, w_ref, o_ref):
    o_ref[...] = _dot(m_ref[...].astype(BF16), w_ref[...]).astype(BF16)


def _mem_kv(mem, wkv):
    rows = mem.shape[0]
    tn = D_MODEL
    return pl.pallas_call(
        _mem_kv_kernel,
        grid=(rows // MEM_TOKENS, wkv.shape[1] // tn),
        in_specs=[pl.BlockSpec((MEM_TOKENS, D_MODEL), lambda i, j: (i, 0)),
                  pl.BlockSpec((D_MODEL, tn), lambda i, j: (0, j))],
        out_specs=pl.BlockSpec((MEM_TOKENS, tn), lambda i, j: (i, j)),
        out_shape=jax.ShapeDtypeStruct((rows, wkv.shape[1]), BF16),
        name="mem_kv",
        compiler_params=_params("parallel", "parallel"),
    )(mem, wkv)


def _xattn_kernel(h1_ref, kv_ref, wq_ref, wo_ref, g_ref, b_ref, wrh_ref, wrl_ref, h2_ref, aff_ref):
    h1 = h1_ref[...]
    q = _dot(h1.astype(BF16), wq_ref[...]).astype(BF16)
    outs = []
    for h in range(MEM_HEADS):
        sl = slice(h * MEM_HEAD_DIM, (h + 1) * MEM_HEAD_DIM)
        kh = kv_ref[:, sl]
        vh = kv_ref[:, D_MODEL + h * MEM_HEAD_DIM:D_MODEL + (h + 1) * MEM_HEAD_DIM]
        s = _dot_nt(q[:, sl], kh) * (MEM_HEAD_DIM ** -0.5)
        m = jnp.max(s, axis=-1, keepdims=True)
        e = jnp.exp(s - m)
        p = (e / jnp.sum(e, axis=-1, keepdims=True)).astype(BF16)
        outs.append(_dot(p, vh))
    o = jnp.concatenate(outs, axis=-1).astype(BF16)
    h2 = _layer_norm(DEEPNORM_ALPHA * h1 + _dot(o, wo_ref[...]), g_ref[...], b_ref[...])
    h2_ref[...] = h2
    hh = h2.astype(BF16)
    hl = (h2 - hh.astype(F32)).astype(BF16)
    logits = _dot_nt(wrh_ref[...], hh) + _dot_nt(wrh_ref[...], hl) + _dot_nt(wrl_ref[...], hh)
    m = jnp.max(logits, axis=0, keepdims=True)
    e = jnp.exp(logits - m)
    aff_ref[...] = e / jnp.sum(e, axis=0, keepdims=True)


def _xattn(h1, kv, wq, wo, g, b, wrh, wrl, batch, seq):
    n = batch * seq
    tm = TOK_TILE
    nt = seq // tm
    row = pl.BlockSpec((tm, D_MODEL), lambda i: (i, 0))
    return pl.pallas_call(
        _xattn_kernel,
        grid=(n // tm,),
        in_specs=[row, pl.BlockSpec((MEM_TOKENS, 2 * D_MODEL), lambda i: (i // nt, 0)),
                  _const_spec(wq.shape), _const_spec(wo.shape), _const_spec(g.shape), _const_spec(b.shape),
                  _const_spec(wrh.shape), _const_spec(wrl.shape)],
        out_specs=(row, pl.BlockSpec((N_EXPERTS, tm), lambda i: (0, i))),
        out_shape=(jax.ShapeDtypeStruct((n, D_MODEL), F32), jax.ShapeDtypeStruct((N_EXPERTS, n), F32)),
        name="xattn_ln2_router",
        compiler_params=_params("parallel"),
    )(h1, kv, wq, wo, g, b, wrh, wrl)


def _route_kernel(aff_ref, idx_ref, gate_ref, pos_ref, bst_ref, *, cap, pchunk):
    a = aff_ref[0]
    nb = a.shape[0]
    bits = lax.bitcast_convert_type(a, I32)

    def total(x):
        return jnp.sum(jnp.sum(x, axis=0, keepdims=True), axis=1, keepdims=True)

    def search(_, carry):
        lo, hi = carry
        mid = lo + ((hi - lo + 1) >> 1)
        ok = total((bits >= mid).astype(F32)) >= cap
        return jnp.where(ok, mid, lo), jnp.where(ok, hi, mid - 1)

    lo0 = jnp.zeros((1, 1), I32)
    hi0 = jnp.full((1, 1), 0x7F800000, I32)
    thr, _ = lax.fori_loop(0, 31, search, (lo0, hi0))

    li = lax.broadcasted_iota(I32, (LANES, LANES), 0)
    lj = lax.broadcasted_iota(I32, (LANES, LANES), 1)
    upper = (li <= lj).astype(BF16)
    bi = lax.broadcasted_iota(I32, (nb, nb), 0)
    bj = lax.broadcasted_iota(I32, (nb, nb), 1)
    strict = (bj < bi).astype(BF16)
    incl = (bi <= bj).astype(BF16)
    ones8 = jnp.ones((8, LANES), BF16)

    def prefix(mask_b):
        rowcum = _dot(mask_b, upper)
        rowtot = jnp.broadcast_to(rowcum[:, LANES - 1:LANES], (nb, LANES)).astype(BF16)
        return rowcum + _dot(strict, rowtot)

    gt = bits > thr
    eq = bits == thr
    need = cap - total(gt.astype(F32))
    sel = jnp.logical_or(gt, jnp.logical_and(eq, prefix(eq.astype(BF16)) <= need))
    selb = sel.astype(BF16)
    cnt = prefix(selb)

    rt_row = _dot_nt(ones8, selb)
    bend = _dot(rt_row.astype(BF16), incl)
    bstart = bend - rt_row
    pos_ref[0] = jnp.where(sel, cnt - 1.0, -1.0).astype(I32)
    bst_ref[0] = bstart[0:1, :].astype(I32)

    cnt_hi = jnp.floor(cnt * (1.0 / 64.0))
    cnt_lo = cnt - 64.0 * cnt_hi
    a_hi, a_mid, a_lo = _split3(a)
    rowid = lax.broadcasted_iota(I32, (nb, LANES), 0).astype(BF16)
    rhs = jnp.concatenate([cnt_hi.astype(BF16), cnt_lo.astype(BF16), selb, rowid, a_hi, a_mid, a_lo], axis=1)
    lanef = lax.broadcasted_iota(I32, (pchunk, LANES), 1).astype(F32)

    for pc in range(cap // pchunk):
        p = (lax.broadcasted_iota(I32, (pchunk, nb), 0) + pc * pchunk).astype(F32)
        g = jnp.logical_and(bstart[0:1, :] <= p, p < bend[0:1, :]).astype(BF16)
        rows = _dot(g, rhs)
        part = lambda k: rows[:, k * LANES:(k + 1) * LANES]
        crow = part(0) * 64.0 + part(1)
        p1 = (lax.broadcasted_iota(I32, (pchunk, LANES), 0) + (pc * pchunk + 1)).astype(F32)
        oh = jnp.logical_and(crow == p1, part(2) > 0.5)
        pick = lambda x: jnp.where(oh, x, 0.0)
        tok_row = _dot_nt(ones8, pick(part(3)).astype(BF16))
        tok_lane = _dot_nt(ones8, pick(lanef).astype(BF16))
        idx_ref[0, :, pc * pchunk:(pc + 1) * pchunk] = (tok_row[0:1] * float(LANES) + tok_lane[0:1]).astype(I32)
        gate = jnp.sum(pick(part(4) + part(5) + part(6)), axis=1, keepdims=True)
        gate_ref[pc * pchunk:(pc + 1) * pchunk, :] = jnp.broadcast_to(gate, (pchunk, LANES))


def _route(aff_t, cap):
    e, n = aff_t.shape
    nb = n // LANES
    assert nb <= 256 and cap % 64 == 0 and cap // 64 <= 256
    pchunk = min(1024, cap)
    aff3 = aff_t.reshape(e, nb, LANES)
    return pl.pallas_call(
        functools.partial(_route_kernel, cap=cap, pchunk=pchunk),
        grid=(e,),
        in_specs=[pl.BlockSpec((1, nb, LANES), lambda i: (i, 0, 0))],
        out_specs=(pl.BlockSpec((1, 1, cap), lambda i: (i, 0, 0)),
                   pl.BlockSpec((cap, LANES), lambda i: (i, 0)),
                   pl.BlockSpec((1, nb, LANES), lambda i: (i, 0, 0)),
                   pl.BlockSpec((1, 1, nb), lambda i: (i, 0, 0))),
        out_shape=(jax.ShapeDtypeStruct((e, 1, cap), I32), jax.ShapeDtypeStruct((e * cap, LANES), F32),
                   jax.ShapeDtypeStruct((e, nb, LANES), I32), jax.ShapeDtypeStruct((e, 1, nb), I32)),
        name="route",
        compiler_params=_params("parallel"),
    )(aff3)


def _ffn_kernel(idx_ref, nxt_ref, gate_ref, wg_ref, wu_ref, wd_ref, h2_hbm, ye_ref, xbuf, sem):
    s = pl.program_id(0)
    slot = s % 2

    def gather(rows_ref, dst_slot):
        def body(i, carry):
            pltpu.make_async_copy(h2_hbm.at[pl.ds(rows_ref[0, 0, i], 1)], xbuf.at[dst_slot, pl.ds(i, 1)],
                                  sem.at[dst_slot]).start()
            return carry
        lax.fori_loop(0, FFN_ROWS, body, 0, unroll=8)

    @pl.when(s == 0)
    def _():
        gather(idx_ref, 0)

    @pl.when(s + 1 < pl.num_programs(0))
    def _():
        gather(nxt_ref, 1 - slot)

    pltpu.make_async_copy(h2_hbm.at[pl.ds(0, FFN_ROWS)], xbuf.at[slot], sem.at[slot]).wait()
    x = xbuf[slot].astype(BF16)
    a = _dot(x, wg_ref[0])
    u = _dot(x, wu_ref[0])
    hmid = (a * jax.nn.sigmoid(a) * u).astype(BF16)
    ye_ref[...] = _dot(hmid, wd_ref[0]) * gate_ref[:, 0:1]


def _ffn(idx, gate, wg, wu, wd, h2, cap):
    e = idx.shape[0]
    tr = FFN_ROWS
    per = cap // tr
    nsteps = e * per
    assert cap % tr == 0
    wspec = lambda shp: pl.BlockSpec((1,) + shp, lambda s: (s // per, 0, 0))
    rows = lambda shift: pl.BlockSpec(
        (1, 1, tr), lambda s: (jnp.minimum(s + shift, nsteps - 1) // per, 0, jnp.minimum(s + shift, nsteps - 1) % per),
        memory_space=pltpu.SMEM)
    return pl.pallas_call(
        _ffn_kernel,
        grid=(nsteps,),
        in_specs=[rows(0), rows(1), pl.BlockSpec((tr, LANES), lambda s: (s, 0)),
                  wspec((D_MODEL, D_FF)), wspec((D_MODEL, D_FF)), wspec((D_FF, D_MODEL)),
                  pl.BlockSpec(memory_space=pl.ANY)],
        out_specs=pl.BlockSpec((tr, D_MODEL), lambda s: (s, 0)),
        out_shape=jax.ShapeDtypeStruct((e * cap, D_MODEL), F32),
        scratch_shapes=[pltpu.VMEM((2, tr, D_MODEL), F32), pltpu.SemaphoreType.DMA((2,))],
        name="ffn",
        compiler_params=_params("arbitrary"),
    )(idx, idx, gate, wg, wu, wd, h2)


def _combine_kernel(bst_ref, pos_ref, h2_ref, g_ref, b_ref, ye_hbm, o_ref, ybuf, sem, *, cap):
    e_n = N_EXPERTS
    w = CMB_WIN
    t = pl.program_id(0)
    blk = t * (CMB_TILE // LANES)
    starts = [bst_ref[e, blk] for e in range(e_n)]
    counts = [bst_ref[e, blk + CMB_TILE // LANES] - starts[e] for e in range(e_n)]
    step = w - SUBLANES
    nrounds = functools.reduce(jnp.maximum, [(c + (step - 1)) // step for c in counts])
    wi = lax.broadcasted_iota(I32, (w, CMB_TILE), 0)

    def round_body(r, y):
        lo = [starts[e] + r * step for e in range(e_n)]
        src = [pl.multiple_of(jnp.minimum(e * cap + (lo[e] // SUBLANES) * SUBLANES, e_n * cap - w), SUBLANES)
               for e in range(e_n)]
        copies = [pltpu.make_async_copy(ye_hbm.at[pl.ds(src[e], w)], ybuf.at[pl.ds(e * w, w)], sem.at[e])
                  for e in range(e_n)]
        for c in copies:
            c.start()
        onehot = []
        for e in range(e_n):
            base = src[e] - e * cap
            p = pos_ref[e:e + 1, :]
            this_round = jnp.logical_and(p >= lo[e], p < lo[e] + step)
            onehot.append(jnp.logical_and(p == base + wi, this_round).astype(BF16))
        oh_t = jnp.concatenate(onehot, axis=0)
        for c in copies:
            c.wait()
        yv = ybuf[...]
        hi = yv.astype(BF16)
        lo_part = (yv - hi.astype(F32)).astype(BF16)
        return y + _dot_tn(oh_t, hi) + _dot_tn(oh_t, lo_part)

    y = lax.fori_loop(0, nrounds, round_body, jnp.zeros((CMB_TILE, D_MODEL), F32))
    o_ref[...] = _layer_norm(DEEPNORM_ALPHA * h2_ref[...] + y, g_ref[...], b_ref[...])


def _combine(bst, pos, h2, g, b, ye, cap):
    n = h2.shape[0]
    tt = CMB_TILE
    assert cap >= CMB_WIN and n % tt == 0
    grid_spec = pltpu.PrefetchScalarGridSpec(
        num_scalar_prefetch=1,
        grid=(n // tt,),
        in_specs=[pl.BlockSpec((N_EXPERTS, tt), lambda i, *_: (0, i)),
                  pl.BlockSpec((tt, D_MODEL), lambda i, *_: (i, 0)),
                  pl.BlockSpec(g.shape, lambda i, *_: (0, 0)), pl.BlockSpec(b.shape, lambda i, *_: (0, 0)),
                  pl.BlockSpec(memory_space=pl.ANY)],
        out_specs=pl.BlockSpec((tt, D_MODEL), lambda i, *_: (i, 0)),
        scratch_shapes=[pltpu.VMEM((N_EXPERTS * CMB_WIN, D_MODEL), F32), pltpu.SemaphoreType.DMA((N_EXPERTS,))],
    )
    return pl.pallas_call(
        functools.partial(_combine_kernel, cap=cap),
        grid_spec=grid_spec,
        out_shape=jax.ShapeDtypeStruct((n, D_MODEL), F32),
        name="combine_ln3",
        compiler_params=_params("arbitrary"),
    )(bst, pos, h2, g, b, ye)


def _prepare(ln_in_g, ln_in_b, w_in, na_rpb, gla_gate_w2, gla_gate_b, gla_norm_g, w_out, ln1_g, ln1_b,
             mem_wq, mem_wkv, mem_wo, ln2_g, ln2_b, w_router, w_gate, w_up, w_down, ln3_g, ln3_b):
    row = lambda v: v.reshape(1, -1).astype(F32)
    w = w_in[0]
    o = np.cumsum((0, NA_WIDTH, NA_WIDTH, NA_WIDTH, GLA_KEY_WIDTH, GLA_KEY_WIDTH, GLA_WIDTH, GLA_WIDTH,
                   2 * GLA_GATE_RANK))
    wgk = jnp.zeros((D_MODEL, LANES), F32).at[:, :2 * GLA_GATE_RANK].set(w[:, o[7]:o[8]])
    w2 = jnp.zeros((LANES, 2 * GLA_KEY_WIDTH), F32)
    for s in range(2):
        w2 = w2.at[s * GLA_GATE_RANK:(s + 1) * GLA_GATE_RANK,
                   s * GLA_KEY_WIDTH:(s + 1) * GLA_KEY_WIDTH].set(gla_gate_w2[0, s])
    tab, rmask = _na_tables(na_rpb[0])
    wr_t = w_router[0].T.astype(F32)
    wr_hi = wr_t.astype(BF16)
    wr_lo = (wr_t - wr_hi.astype(F32)).astype(BF16)
    return dict(
        ln_in=(row(ln_in_g), row(ln_in_b)),
        wna=w[:, o[0]:o[3]].astype(BF16), wqk=w[:, o[3]:o[5]].astype(BF16), wv=w[:, o[5]:o[6]].astype(BF16),
        wr=w[:, o[6]:o[7]].astype(BF16), wgk=wgk.astype(BF16), w2=w2.astype(BF16),
        gb=gla_gate_b[0].reshape(1, -1).astype(F32),
        tab=tab, rmask=rmask, ng=row(gla_norm_g[0]),
        wo1=w_out[0][:NA_WIDTH].astype(BF16), wo2=w_out[0][NA_WIDTH:].astype(BF16),
        ln1=(row(ln1_g[0]), row(ln1_b[0])),
        wq=mem_wq[0].astype(BF16), wkv=mem_wkv[0].astype(BF16), wmo=mem_wo[0].astype(BF16),
        ln2=(row(ln2_g[0]), row(ln2_b[0])),
        wr_hi=wr_hi, wr_lo=wr_lo,
        wg=w_gate[0].astype(BF16), wu=w_up[0].astype(BF16), wd=w_down[0].astype(BF16),
        ln3=(row(ln3_g[0]), row(ln3_b[0])),
    )


def _trunk(x, mem, p):
    batch, seq, _ = x.shape
    n = batch * seq
    cap = EC_CAPACITY_FACTOR * n // N_EXPERTS
    h0, qkv, qk, v, r, la = _in_proj(x.reshape(n, D_MODEL), *p["ln_in"], p["wna"], p["wqk"], p["wv"], p["wr"],
                                     p["wgk"], p["w2"], p["gb"])
    na = _na(qkv, p["tab"], p["rmask"], batch, seq)
    of, ob = _gla(qk, v, la, batch, seq)
    h1 = _mix_out(na, of, ob, r, p["ng"], h0, p["wo1"], p["wo2"], *p["ln1"])
    kv = _mem_kv(mem.reshape(batch * MEM_TOKENS, D_MODEL), p["wkv"])
    h2, aff_t = _xattn(h1, kv, p["wq"], p["wmo"], *p["ln2"], p["wr_hi"], p["wr_lo"], batch, seq)
    idx, gate, pos, bst = _route(aff_t, cap)
    ye = _ffn(idx, gate, p["wg"], p["wu"], p["wd"], h2, cap)
    bst = jnp.concatenate([bst[:, 0, :], jnp.full((N_EXPERTS, 1), cap, I32)], axis=1)
    out = _combine(bst, pos.reshape(N_EXPERTS, n), h2, *p["ln3"], ye, cap)
    return out.reshape(batch, seq, D_MODEL)


def kernel(x_prompt, x_sample, mem_prompt, mem_sample, ln_in_g, ln_in_b, w_in, na_rpb, gla_gate_w2, gla_gate_b,
           gla_norm_g, w_out, ln1_g, ln1_b, mem_wq, mem_wkv, mem_wo, ln2_g, ln2_b, w_router, w_gate, w_up, w_down,
           ln3_g, ln3_b):
    p = _prepare(ln_in_g, ln_in_b, w_in, na_rpb, gla_gate_w2, gla_gate_b, gla_norm_g, w_out, ln1_g, ln1_b,
                 mem_wq, mem_wkv, mem_wo, ln2_g, ln2_b, w_router, w_gate, w_up, w_down, ln3_g, ln3_b)
    return _trunk(x_prompt, mem_prompt, p), _trunk(x_sample, mem_sample, p)
```

```python
import functools

import numpy as np
import jax
import jax.numpy as jnp
from jax import lax
from jax.experimental import pallas as pl
from jax.experimental.pallas import tpu as pltpu

F32 = jnp.float32
BF16 = jnp.bfloat16
I32 = jnp.int32

D_MODEL = 1024
GRID_W = 64
NA_HEADS = 8
NA_HEAD_DIM = 64
NA_WIDTH = NA_HEADS * NA_HEAD_DIM
NA_WIN_ROWS = 8
NA_WIN_COLS = 16
GLA_HEADS = 4
GLA_DK = 64
GLA_DV = 128
GLA_KEY_WIDTH = GLA_HEADS * GLA_DK
GLA_WIDTH = GLA_HEADS * GLA_DV
GLA_GATE_RANK = 16
GLA_GATE_NORM = 16.0
GLA_CHUNK = 64
MEM_TOKENS = 256
MEM_HEADS = 4
MEM_HEAD_DIM = D_MODEL // MEM_HEADS
N_EXPERTS = 16
EC_CAPACITY_FACTOR = 2
D_FF = 2 * D_MODEL
LN_EPS = 1e-5
RMS_EPS = 1e-5
DEPTH = 1
DEEPNORM_ALPHA = (2 * DEPTH) ** 0.25

LANES = 128
SUBLANES = 8
NEG = -1e30
VMEM_LIMIT = 56 * 1024 * 1024

NA_QROWS = 8
NA_SLAB = 16
GLA_SUPER = 512
TOK_TILE = 512
FFN_ROWS = 512
CMB_TILE = 256
CMB_WIN = 64


def _dot(a, b):
    return jnp.dot(a, b, preferred_element_type=F32)


def _dot_nt(a, b):
    return lax.dot_general(a, b, (((1,), (1,)), ((), ())), preferred_element_type=F32)


def _dot_tn(a, b):
    return lax.dot_general(a, b, (((0,), (0,)), ((), ())), preferred_element_type=F32)


def _layer_norm(x, g, b):
    mu = jnp.mean(x, axis=-1, keepdims=True)
    xc = x - mu
    var = jnp.mean(xc * xc, axis=-1, keepdims=True)
    return xc * lax.rsqrt(var + LN_EPS) * g + b


def _split3(x):
    hi = x.astype(BF16)
    r1 = x - hi.astype(F32)
    mid = r1.astype(BF16)
    lo = (r1 - mid.astype(F32)).astype(BF16)
    return hi, mid, lo


def _params(*sem):
    return pltpu.CompilerParams(dimension_semantics=sem, vmem_limit_bytes=VMEM_LIMIT)


def _const_spec(shape):
    nd = len(shape)
    return pl.BlockSpec(shape, lambda *_: (0,) * nd)


def _in_proj_kernel(x_ref, g_ref, b_ref, wna_ref, wqk_ref, wv_ref, wr_ref, wgk_ref, w2_ref, gb_ref,
                    h0_ref, qkv_ref, qk_ref, v_ref, r_ref, la_ref):
    h0 = _layer_norm(x_ref[...], g_ref[...], b_ref[...])
    h0_ref[...] = h0
    hb = h0.astype(BF16)
    qkv_ref[...] = _dot(hb, wna_ref[...]).astype(BF16)
    qk_ref[...] = _dot(hb, wqk_ref[...])
    v_ref[...] = _dot(hb, wv_ref[...]).astype(BF16)
    r_ref[...] = _dot(hb, wr_ref[...])
    gk = _dot(hb, wgk_ref[...])
    z = _dot(gk.astype(BF16), w2_ref[...]) + gb_ref[...]
    log_sig = jnp.minimum(z, 0.0) - jnp.log1p(jnp.exp(-jnp.abs(z)))
    la_ref[...] = log_sig * (1.0 / GLA_GATE_NORM)


def _in_proj(x, g, b, wna, wqk, wv, wr, wgk, w2, gb):
    n = x.shape[0]
    tm = TOK_TILE
    row = lambda w: pl.BlockSpec((tm, w), lambda i: (i, 0))
    outs = (
        jax.ShapeDtypeStruct((n, D_MODEL), F32),
        jax.ShapeDtypeStruct((n, 3 * NA_WIDTH), BF16),
        jax.ShapeDtypeStruct((n, 2 * GLA_KEY_WIDTH), F32),
        jax.ShapeDtypeStruct((n, GLA_WIDTH), BF16),
        jax.ShapeDtypeStruct((n, GLA_WIDTH), F32),
        jax.ShapeDtypeStruct((n, 2 * GLA_KEY_WIDTH), F32),
    )
    return pl.pallas_call(
        _in_proj_kernel,
        grid=(n // tm,),
        in_specs=[row(D_MODEL), _const_spec(g.shape), _const_spec(b.shape), _const_spec(wna.shape),
                  _const_spec(wqk.shape), _const_spec(wv.shape), _const_spec(wr.shape),
                  _const_spec(wgk.shape), _const_spec(w2.shape), _const_spec(gb.shape)],
        out_specs=(row(D_MODEL), row(3 * NA_WIDTH), row(2 * GLA_KEY_WIDTH), row(GLA_WIDTH), row(GLA_WIDTH),
                   row(2 * GLA_KEY_WIDTH)),
        out_shape=outs,
        name="ln_in_proj",
        compiler_params=_params("parallel"),
    )(x, g, b, wna, wqk, wv, wr, wgk, w2, gb)


def _na_tables(rpb):
    kw = NA_WIN_COLS
    cols = np.arange(GRID_W)
    col_start = np.clip(cols - kw // 2, 0, GRID_W - kw)
    j = np.arange(GRID_W)
    col_ok = (j[None, :] >= col_start[:, None]) & (j[None, :] < col_start[:, None] + kw)
    col_off = np.clip(j[None, :] - cols[:, None] + (NA_WIN_COLS - 1), 0, 2 * NA_WIN_COLS - 2)
    nblk = NA_SLAB + NA_QROWS
    ro = np.arange(nblk) - 5
    ro_ok = (ro >= 0) & (ro < 2 * NA_WIN_ROWS - 1)
    ro_c = np.clip(ro, 0, 2 * NA_WIN_ROWS - 2)
    t = rpb[:, ro_c][:, :, col_off]
    ok = jnp.asarray(ro_ok[None, :, None, None] & col_ok[None, None])
    t = jnp.where(ok, t, NEG).astype(F32)
    flat = jnp.transpose(t, (0, 2, 1, 3)).reshape(NA_HEADS, GRID_W, nblk * GRID_W)
    shifted = jnp.concatenate([flat[:, :, GRID_W:], jnp.full((NA_HEADS, GRID_W, GRID_W), NEG, F32)], axis=-1)
    tab = jnp.stack([flat, shifted])

    a = np.arange(NA_QROWS)
    i = np.arange(NA_SLAB)
    masks = []
    for w in (np.maximum(a, 4), a, np.minimum(a, 4)):
        valid = (i[None, :] >= w[:, None]) & (i[None, :] < w[:, None] + NA_WIN_ROWS)
        m = np.where(valid, 0.0, NEG).astype(np.float32)
        masks.append(np.repeat(np.repeat(m, GRID_W, axis=0), GRID_W, axis=1))
    return tab, jnp.asarray(np.stack(masks))


def _na_kernel(q_ref, kp_ref, kc_ref, kn_ref, vp_ref, vc_ref, vn_ref, tab_ref, rm_ref, o_ref):
    half = NA_QROWS * GRID_W // 2
    k = jnp.concatenate([kp_ref[half:, :], kc_ref[...], kn_ref[:half, :]], axis=0)
    v = jnp.concatenate([vp_ref[half:, :], vc_ref[...], vn_ref[:half, :]], axis=0)
    rm = rm_ref[0]
    nq = NA_QROWS * GRID_W
    nk = NA_SLAB * GRID_W
    lane = lax.broadcasted_iota(I32, (nq, LANES), 1)
    first = lane < NA_HEAD_DIM
    for hp in range(NA_HEADS // 2):
        sl = slice(hp * LANES, (hp + 1) * LANES)
        q2 = q_ref[:, sl]
        k2 = k[:, sl]
        v2 = v[:, sl]
        pv = []
        for sub in range(2):
            h = 2 * hp + sub
            qm = jnp.where(first if sub == 0 else jnp.logical_not(first), q2, jnp.zeros_like(q2))
            s = _dot_nt(qm, k2) * (NA_HEAD_DIM ** -0.5)
            strips = []
            for a in range(NA_QROWS):
                par = a % 2
                off = ((8 - a) if par == 0 else (7 - a)) * GRID_W
                strips.append(s[a * GRID_W:(a + 1) * GRID_W, :] + tab_ref[par, h, :, off:off + nk])
            s = jnp.concatenate(strips, axis=0) + rm
            m = jnp.max(s, axis=-1, keepdims=True)
            e = jnp.exp(s - m)
            l = jnp.sum(e, axis=-1, keepdims=True)
            pv.append(_dot(e.astype(BF16), v2) / l)
        o_ref[:, sl] = jnp.where(first, pv[0], pv[1]).astype(BF16)


def _na(qkv, tab, rmask, batch, seq):
    n = batch * seq
    nq = NA_QROWS * GRID_W
    nblk = seq // nq
    assert seq % nq == 0 and nblk >= 2
    cur = lambda c: pl.BlockSpec((nq, NA_WIDTH), lambda b, r: (b * nblk + r, c))
    prv = lambda c: pl.BlockSpec((nq, NA_WIDTH), lambda b, r: (b * nblk + jnp.maximum(r - 1, 0), c))
    nxt = lambda c: pl.BlockSpec((nq, NA_WIDTH), lambda b, r: (b * nblk + jnp.minimum(r + 1, nblk - 1), c))
    variant = lambda b, r: (jnp.where(r == 0, 0, jnp.where(r == nblk - 1, 2, 1)), 0, 0)
    return pl.pallas_call(
        _na_kernel,
        grid=(batch, nblk),
        in_specs=[cur(0), prv(1), cur(1), nxt(1), prv(2), cur(2), nxt(2),
                  _const_spec(tab.shape), pl.BlockSpec((1,) + rmask.shape[1:], variant)],
        out_specs=pl.BlockSpec((nq, NA_WIDTH), lambda b, r: (b * nblk + r, 0)),
        out_shape=jax.ShapeDtypeStruct((n, NA_WIDTH), BF16),
        name="nbr_attn",
        compiler_params=_params("parallel", "parallel"),
    )(qkv, qkv, qkv, qkv, qkv, qkv, qkv, tab, rmask)


def _gla_kernel(qkf_ref, vf_ref, laf_ref, qkb_ref, vb_ref, lab_ref, of_ref, ob_ref, sf_ref, sb_ref):
    c = GLA_CHUNK
    kwid = GLA_KEY_WIDTH
    nchunk = GLA_SUPER // c

    @pl.when(pl.program_id(1) == 0)
    def _():
        sf_ref[...] = jnp.zeros_like(sf_ref)
        sb_ref[...] = jnp.zeros_like(sb_ref)

    ri = lax.broadcasted_iota(I32, (c, c), 0)
    ci = lax.broadcasted_iota(I32, (c, c), 1)
    tri_f = (ci <= ri).astype(BF16)
    tri_b = (ci >= ri).astype(BF16)
    ar = lax.broadcasted_iota(I32, (c, kwid), 0)
    ac = lax.broadcasted_iota(I32, (c, kwid), 1) % c
    keep_f = ac <= ar
    keep_b = ac >= ar
    wr_ = lax.broadcasted_iota(I32, (kwid, kwid), 0) // c
    wc_ = lax.broadcasted_iota(I32, (kwid, kwid), 1) // GLA_DK
    wmask = wr_ == wc_
    vr_ = lax.broadcasted_iota(I32, (kwid, GLA_WIDTH), 0) // c
    vc_ = lax.broadcasted_iota(I32, (kwid, GLA_WIDTH), 1) // GLA_DV
    vmask = vr_ == vc_
    sr_ = lax.broadcasted_iota(I32, (GLA_WIDTH, kwid), 0) // GLA_DV
    sc_ = lax.broadcasted_iota(I32, (GLA_WIDTH, kwid), 1) // GLA_DK
    smask = sr_ == sc_

    def unit(qk, v, la, tri, keep, last, st_ref):
        q = qk[:, :kwid] * (GLA_DK ** -0.5)
        k = qk[:, kwid:]
        hi, mid, lo = _split3(la)
        b = _dot(tri, hi) + _dot(tri, mid) + _dot(tri, lo)
        bl = b[last:last + 1, :]
        qe = (q * jnp.exp(b)).astype(BF16)
        ke = (k * jnp.exp(-b)).astype(BF16)
        kd = (k * jnp.exp(bl - b)).astype(BF16)
        dec = jnp.exp(bl)
        wt = jnp.where(wmask, jnp.concatenate([ke] * GLA_HEADS, axis=0), jnp.zeros((kwid, kwid), BF16))
        a = jnp.where(keep, _dot_nt(qe, wt), 0.0)
        vbd = jnp.where(vmask, jnp.concatenate([v] * GLA_HEADS, axis=0), jnp.zeros((kwid, GLA_WIDTH), BF16))
        st = st_ref[...]
        o = _dot(a.astype(BF16), vbd) + _dot_nt(qe, st.astype(BF16))
        upd = _dot_tn(v, kd)
        st_ref[...] = st * dec + jnp.where(smask, upd, 0.0)
        return o

    def body(ic, carry):
        rf = pl.multiple_of(ic * c, c)
        rb = pl.multiple_of((nchunk - 1 - ic) * c, c)
        of_ref[pl.ds(rf, c), :] = unit(qkf_ref[pl.ds(rf, c), :], vf_ref[pl.ds(rf, c), :],
                                       laf_ref[pl.ds(rf, c), :], tri_f, keep_f, c - 1, sf_ref)
        ob_ref[pl.ds(rb, c), :] = unit(qkb_ref[pl.ds(rb, c), :], vb_ref[pl.ds(rb, c), :],
                                       lab_ref[pl.ds(rb, c), :], tri_b, keep_b, 0, sb_ref)
        return carry

    lax.fori_loop(0, nchunk, body, 0, unroll=2)


def _gla(qk, v, la, batch, seq):
    n = batch * seq
    ns = seq // GLA_SUPER
    assert seq % GLA_SUPER == 0
    fwd = lambda w, c: pl.BlockSpec((GLA_SUPER, w), lambda b, s: (b * ns + s, c))
    bwd = lambda w, c: pl.BlockSpec((GLA_SUPER, w), lambda b, s: (b * ns + ns - 1 - s, c))
    kw2 = 2 * GLA_KEY_WIDTH
    return pl.pallas_call(
        _gla_kernel,
        grid=(batch, ns),
        in_specs=[fwd(kw2, 0), fwd(GLA_WIDTH, 0), fwd(GLA_KEY_WIDTH, 0),
                  bwd(kw2, 0), bwd(GLA_WIDTH, 0), bwd(GLA_KEY_WIDTH, 1)],
        out_specs=(fwd(GLA_WIDTH, 0), bwd(GLA_WIDTH, 0)),
        out_shape=(jax.ShapeDtypeStruct((n, GLA_WIDTH), F32), jax.ShapeDtypeStruct((n, GLA_WIDTH), F32)),
        scratch_shapes=[pltpu.VMEM((GLA_WIDTH, GLA_KEY_WIDTH), F32), pltpu.VMEM((GLA_WIDTH, GLA_KEY_WIDTH), F32)],
        name="gla",
        compiler_params=_params("parallel", "arbitrary"),
    )(qk, v, la, qk, v, la)


def _mix_out_kernel(na_ref, of_ref, ob_ref, r_ref, ng_ref, h0_ref, w1_ref, w2_ref, g_ref, b_ref, h1_ref):
    o = of_ref[...] + ob_ref[...]
    parts = []
    for h in range(GLA_HEADS):
        oh = o[:, h * GLA_DV:(h + 1) * GLA_DV]
        ms = jnp.mean(oh * oh, axis=-1, keepdims=True)
        parts.append(oh * lax.rsqrt(ms + RMS_EPS) * ng_ref[...])
    r = r_ref[...]
    gl = jnp.concatenate(parts, axis=-1) * (r * jax.nn.sigmoid(r))
    mixed = _dot(na_ref[...], w1_ref[...]) + _dot(gl.astype(BF16), w2_ref[...])
    h1_ref[...] = _layer_norm(DEEPNORM_ALPHA * h0_ref[...] + mixed, g_ref[...], b_ref[...])


def _mix_out(na, of, ob, r, ng, h0, w1, w2, g, b):
    n = na.shape[0]
    tm = TOK_TILE
    row = lambda w: pl.BlockSpec((tm, w), lambda i: (i, 0))
    return pl.pallas_call(
        _mix_out_kernel,
        grid=(n // tm,),
        in_specs=[row(NA_WIDTH), row(GLA_WIDTH), row(GLA_WIDTH), row(GLA_WIDTH), _const_spec(ng.shape),
                  row(D_MODEL), _const_spec(w1.shape), _const_spec(w2.shape), _const_spec(g.shape),
                  _const_spec(b.shape)],
        out_specs=row(D_MODEL),
        out_shape=jax.ShapeDtypeStruct((n, D_MODEL), F32),
        name="mix_out_ln1",
        compiler_params=_params("parallel"),
    )(na, of, ob, r, ng, h0, w1, w2, g, b)


def _mem_kv_kernel(m_ref, w_ref, o_ref):
    o_ref[...] = _dot(m_ref[...].astype(BF16), w_ref[...]).astype(BF16)


def _mem_kv(mem, wkv):
    rows = mem.shape[0]
    tn = D_MODEL
    return pl.pallas_call(
        _mem_kv_kernel,
        grid=(rows // MEM_TOKENS, wkv.shape[1] // tn),
        in_specs=[pl.BlockSpec((MEM_TOKENS, D_MODEL), lambda i, j: (i, 0)),
                  pl.BlockSpec((D_MODEL, tn), lambda i, j: (0, j))],
        out_specs=pl.BlockSpec((MEM_TOKENS, tn), lambda i, j: (i, j)),
        out_shape=jax.ShapeDtypeStruct((rows, wkv.shape[1]), BF16),
        name="mem_kv",
        compiler_params=_params("parallel", "parallel"),
    )(mem, wkv)


def _xattn_kernel(h1_ref, kv_ref, wq_ref, wo_ref, g_ref, b_ref, wrh_ref, wrl_ref, h2_ref, h2t_ref, aff_ref):
    h1 = h1_ref[...]
    q = _dot(h1.astype(BF16), wq_ref[...]).astype(BF16)
    outs = []
    for h in range(MEM_HEADS):
        sl = slice(h * MEM_HEAD_DIM, (h + 1) * MEM_HEAD_DIM)
        kh = kv_ref[:, sl]
        vh = kv_ref[:, D_MODEL + h * MEM_HEAD_DIM:D_MODEL + (h + 1) * MEM_HEAD_DIM]
        s = _dot_nt(q[:, sl], kh) * (MEM_HEAD_DIM ** -0.5)
        m = jnp.max(s, axis=-1, keepdims=True)
        e = jnp.exp(s - m)
        p = (e / jnp.sum(e, axis=-1, keepdims=True)).astype(BF16)
        outs.append(_dot(p, vh))
    o = jnp.concatenate(outs, axis=-1).astype(BF16)
    h2 = _layer_norm(DEEPNORM_ALPHA * h1 + _dot(o, wo_ref[...]), g_ref[...], b_ref[...])
    h2_ref[...] = h2
    for c in range(D_MODEL // LANES):
        h2t_ref[:, c, :] = h2[:, c * LANES:(c + 1) * LANES]
    hh = h2.astype(BF16)
    hl = (h2 - hh.astype(F32)).astype(BF16)
    logits = _dot_nt(wrh_ref[...], hh) + _dot_nt(wrh_ref[...], hl) + _dot_nt(wrl_ref[...], hh)
    m = jnp.max(logits, axis=0, keepdims=True)
    e = jnp.exp(logits - m)
    aff_ref[...] = e / jnp.sum(e, axis=0, keepdims=True)


def _xattn(h1, kv, wq, wo, g, b, wrh, wrl, batch, seq):
    n = batch * seq
    tm = TOK_TILE
    nt = seq // tm
    row = pl.BlockSpec((tm, D_MODEL), lambda i: (i, 0))
    return pl.pallas_call(
        _xattn_kernel,
        grid=(n // tm,),
        in_specs=[row, pl.BlockSpec((MEM_TOKENS, 2 * D_MODEL), lambda i: (i // nt, 0)),
                  _const_spec(wq.shape), _const_spec(wo.shape), _const_spec(g.shape), _const_spec(b.shape),
                  _const_spec(wrh.shape), _const_spec(wrl.shape)],
        out_specs=(row, pl.BlockSpec((tm, D_MODEL // LANES, LANES), lambda i: (i, 0, 0)),
                   pl.BlockSpec((N_EXPERTS, tm), lambda i: (0, i))),
        out_shape=(jax.ShapeDtypeStruct((n, D_MODEL), F32),
                   jax.ShapeDtypeStruct((n, D_MODEL // LANES, LANES), F32),
                   jax.ShapeDtypeStruct((N_EXPERTS, n), F32)),
        name="xattn_ln2_router",
        compiler_params=_params("parallel"),
    )(h1, kv, wq, wo, g, b, wrh, wrl)


def _route_kernel(aff_ref, idx_ref, gate_ref, pos_ref, bst_ref, *, cap, pchunk):
    a = aff_ref[0]
    nb = a.shape[0]
    bits = lax.bitcast_convert_type(a, I32)

    def total(x):
        return jnp.sum(jnp.sum(x, axis=0, keepdims=True), axis=1, keepdims=True)

    def search(_, carry):
        lo, hi = carry
        mid = lo + ((hi - lo + 1) >> 1)
        ok = total((bits >= mid).astype(F32)) >= cap
        return jnp.where(ok, mid, lo), jnp.where(ok, hi, mid - 1)

    lo0 = jnp.zeros((1, 1), I32)
    hi0 = jnp.full((1, 1), 0x7F800000, I32)
    thr, _ = lax.fori_loop(0, 31, search, (lo0, hi0))

    li = lax.broadcasted_iota(I32, (LANES, LANES), 0)
    lj = lax.broadcasted_iota(I32, (LANES, LANES), 1)
    upper = (li <= lj).astype(BF16)
    bi = lax.broadcasted_iota(I32, (nb, nb), 0)
    bj = lax.broadcasted_iota(I32, (nb, nb), 1)
    strict = (bj < bi).astype(BF16)
    incl = (bi <= bj).astype(BF16)
    ones8 = jnp.ones((8, LANES), BF16)

    def prefix(mask_b):
        rowcum = _dot(mask_b, upper)
        rowtot = jnp.broadcast_to(rowcum[:, LANES - 1:LANES], (nb, LANES)).astype(BF16)
        return rowcum + _dot(strict, rowtot)

    gt = bits > thr
    eq = bits == thr
    need = cap - total(gt.astype(F32))
    sel = jnp.logical_or(gt, jnp.logical_and(eq, prefix(eq.astype(BF16)) <= need))
    selb = sel.astype(BF16)
    cnt = prefix(selb)

    rt_row = _dot_nt(ones8, selb)
    bend = _dot(rt_row.astype(BF16), incl)
    bstart = bend - rt_row
    pos_ref[0] = jnp.where(sel, cnt - 1.0, -1.0).astype(I32)
    bst_ref[0] = bstart[0:1, :].astype(I32)

    cnt_hi = jnp.floor(cnt * (1.0 / 64.0))
    cnt_lo = cnt - 64.0 * cnt_hi
    a_hi, a_mid, a_lo = _split3(a)
    rowid = lax.broadcasted_iota(I32, (nb, LANES), 0).astype(BF16)
    rhs = jnp.concatenate([cnt_hi.astype(BF16), cnt_lo.astype(BF16), selb, rowid, a_hi, a_mid, a_lo], axis=1)
    lanef = lax.broadcasted_iota(I32, (pchunk, LANES), 1).astype(F32)

    for pc in range(cap // pchunk):
        p = (lax.broadcasted_iota(I32, (pchunk, nb), 0) + pc * pchunk).astype(F32)
        g = jnp.logical_and(bstart[0:1, :] <= p, p < bend[0:1, :]).astype(BF16)
        rows = _dot(g, rhs)
        part = lambda k: rows[:, k * LANES:(k + 1) * LANES]
        crow = part(0) * 64.0 + part(1)
        p1 = (lax.broadcasted_iota(I32, (pchunk, LANES), 0) + (pc * pchunk + 1)).astype(F32)
        oh = jnp.logical_and(crow == p1, part(2) > 0.5)
        pick = lambda x: jnp.where(oh, x, 0.0)
        tok_row = _dot_nt(ones8, pick(part(3)).astype(BF16))
        tok_lane = _dot_nt(ones8, pick(lanef).astype(BF16))
        idx_ref[0, :, pc * pchunk:(pc + 1) * pchunk] = (tok_row[0:1] * float(LANES) + tok_lane[0:1]).astype(I32)
        gate = jnp.sum(pick(part(4) + part(5) + part(6)), axis=1, keepdims=True)
        gate_ref[pc * pchunk:(pc + 1) * pchunk, :] = jnp.broadcast_to(gate, (pchunk, LANES))


def _route(aff_t, cap):
    e, n = aff_t.shape
    nb = n // LANES
    assert nb <= 256 and cap % 64 == 0 and cap // 64 <= 256
    pchunk = min(1024, cap)
    aff3 = aff_t.reshape(e, nb, LANES)
    return pl.pallas_call(
        functools.partial(_route_kernel, cap=cap, pchunk=pchunk),
        grid=(e,),
        in_specs=[pl.BlockSpec((1, nb, LANES), lambda i: (i, 0, 0))],
        out_specs=(pl.BlockSpec((1, 1, cap), lambda i: (i, 0, 0)),
                   pl.BlockSpec((cap, LANES), lambda i: (i, 0)),
                   pl.BlockSpec((1, nb, LANES), lambda i: (i, 0, 0)),
                   pl.BlockSpec((1, 1, nb), lambda i: (i, 0, 0))),
        out_shape=(jax.ShapeDtypeStruct((e, 1, cap), I32), jax.ShapeDtypeStruct((e * cap, LANES), F32),
                   jax.ShapeDtypeStruct((e, nb, LANES), I32), jax.ShapeDtypeStruct((e, 1, nb), I32)),
        name="route",
        compiler_params=_params("parallel"),
    )(aff3)


def _ffn_kernel(idx_ref, nxt_ref, gate_ref, wg_ref, wu_ref, wd_ref, h2_hbm, ye_ref, xbuf, sem):
    s = pl.program_id(0)
    slot = s % 2

    def row_copy(rows_ref, dst_slot, i):
        return pltpu.make_async_copy(h2_hbm.at[rows_ref[0, 0, i]], xbuf.at[dst_slot, i], sem.at[dst_slot])

    def wait_slot(dst_slot):
        pltpu.make_async_copy(h2_hbm.at[pl.ds(0, FFN_ROWS)], xbuf.at[dst_slot], sem.at[dst_slot]).wait()

    @pl.when(s == 0)
    def _():
        def body(i, carry):
            row_copy(idx_ref, 0, i).start()
            return carry
        lax.fori_loop(0, FFN_ROWS, body, 0, unroll=8)

    wait_slot(slot)
    x = jnp.concatenate([xbuf[slot, :, c, :] for c in range(D_MODEL // LANES)], axis=-1).astype(BF16)
    for i in range(FFN_ROWS):
        row_copy(nxt_ref, 1 - slot, i).start()
    a = _dot(x, wg_ref[0])
    u = _dot(x, wu_ref[0])
    hmid = (a * jax.nn.sigmoid(a) * u).astype(BF16)
    ye_ref[...] = _dot(hmid, wd_ref[0]) * gate_ref[:, 0:1]

    @pl.when(s + 1 == pl.num_programs(0))
    def _():
        wait_slot(1 - slot)


def _ffn(idx, gate, wg, wu, wd, h2, cap):
    e = idx.shape[0]
    tr = FFN_ROWS
    per = cap // tr
    nsteps = e * per
    assert cap % tr == 0
    wspec = lambda shp: pl.BlockSpec((1,) + shp, lambda s: (s // per, 0, 0))
    rows = lambda shift: pl.BlockSpec(
        (1, 1, tr), lambda s: (jnp.minimum(s + shift, nsteps - 1) // per, 0, jnp.minimum(s + shift, nsteps - 1) % per),
        memory_space=pltpu.SMEM)
    return pl.pallas_call(
        _ffn_kernel,
        grid=(nsteps,),
        in_specs=[rows(0), rows(1), pl.BlockSpec((tr, LANES), lambda s: (s, 0)),
                  wspec((D_MODEL, D_FF)), wspec((D_MODEL, D_FF)), wspec((D_FF, D_MODEL)),
                  pl.BlockSpec(memory_space=pl.ANY)],
        out_specs=pl.BlockSpec((tr, D_MODEL), lambda s: (s, 0)),
        out_shape=jax.ShapeDtypeStruct((e * cap, D_MODEL), F32),
        scratch_shapes=[pltpu.VMEM((2, tr, D_MODEL // LANES, LANES), F32), pltpu.SemaphoreType.DMA((2,))],
        name="ffn",
        compiler_params=_params("arbitrary"),
    )(idx, idx, gate, wg, wu, wd, h2)


def _combine_kernel(bst_ref, pos_ref, h2_ref, g_ref, b_ref, ye_hbm, o_ref, ybuf, sem, *, cap):
    e_n = N_EXPERTS
    w = CMB_WIN
    step = w - SUBLANES
    per = CMB_TILE // LANES
    t = pl.program_id(0)
    last = pl.num_programs(0) - 1
    slot = t % 2
    wi = lax.broadcasted_iota(I32, (w, CMB_TILE), 0)

    def windows(tile, r):
        lo = [bst_ref[e, tile * per] + r * step for e in range(e_n)]
        src = [pl.multiple_of(jnp.minimum(e * cap + (lo[e] // SUBLANES) * SUBLANES, e_n * cap - w), SUBLANES)
               for e in range(e_n)]
        return lo, src

    def copies(src, dst_slot):
        return [pltpu.make_async_copy(ye_hbm.at[pl.ds(src[e], w)], ybuf.at[dst_slot, pl.ds(e * w, w)],
                                      sem.at[dst_slot, e]) for e in range(e_n)]

    def contribution(lo, src, buf_slot):
        onehot = []
        for e in range(e_n):
            base = src[e] - e * cap
            p = pos_ref[e:e + 1, :]
            this_round = jnp.logical_and(p >= lo[e], p < lo[e] + step)
            onehot.append(jnp.logical_and(p == base + wi, this_round).astype(BF16))
        oh_t = jnp.concatenate(onehot, axis=0)
        yv = ybuf[buf_slot]
        hi = yv.astype(BF16)
        lo_part = (yv - hi.astype(F32)).astype(BF16)
        return _dot_tn(oh_t, hi) + _dot_tn(oh_t, lo_part)

    lo0, src0 = windows(t, 0)

    @pl.when(t == 0)
    def _():
        for c in copies(src0, 0):
            c.start()

    _, src_next = windows(jnp.minimum(t + 1, last), 0)
    for c in copies(src_next, 1 - slot):
        c.start()
    for c in copies(src0, slot):
        c.wait()
    y = contribution(lo0, src0, slot)

    counts = [bst_ref[e, (t + 1) * per] - bst_ref[e, t * per] for e in range(e_n)]
    nrounds = functools.reduce(jnp.maximum, [(c + (step - 1)) // step for c in counts])

    def extra_round(r, y):
        lo, src = windows(t, r)
        for c in copies(src, slot):
            c.start()
        for c in copies(src, slot):
            c.wait()
        return y + contribution(lo, src, slot)

    y = lax.fori_loop(1, nrounds, extra_round, y)
    o_ref[...] = _layer_norm(DEEPNORM_ALPHA * h2_ref[...] + y, g_ref[...], b_ref[...])

    @pl.when(t == last)
    def _():
        for c in copies(src_next, 1 - slot):
            c.wait()


def _combine(bst, pos, h2, g, b, ye, cap):
    n = h2.shape[0]
    tt = CMB_TILE
    assert cap >= CMB_WIN and n % tt == 0
    grid_spec = pltpu.PrefetchScalarGridSpec(
        num_scalar_prefetch=1,
        grid=(n // tt,),
        in_specs=[pl.BlockSpec((N_EXPERTS, tt), lambda i, *_: (0, i)),
                  pl.BlockSpec((tt, D_MODEL), lambda i, *_: (i, 0)),
                  pl.BlockSpec(g.shape, lambda i, *_: (0, 0)), pl.BlockSpec(b.shape, lambda i, *_: (0, 0)),
                  pl.BlockSpec(memory_space=pl.ANY)],
        out_specs=pl.BlockSpec((tt, D_MODEL), lambda i, *_: (i, 0)),
        scratch_shapes=[pltpu.VMEM((2, N_EXPERTS * CMB_WIN, D_MODEL), F32),
                        pltpu.SemaphoreType.DMA((2, N_EXPERTS))],
    )
    return pl.pallas_call(
        functools.partial(_combine_kernel, cap=cap),
        grid_spec=grid_spec,
        out_shape=jax.ShapeDtypeStruct((n, D_MODEL), F32),
        name="combine_ln3",
        compiler_params=_params("arbitrary"),
    )(bst, pos, h2, g, b, ye)


def _prepare(ln_in_g, ln_in_b, w_in, na_rpb, gla_gate_w2, gla_gate_b, gla_norm_g, w_out, ln1_g, ln1_b,
             mem_wq, mem_wkv, mem_wo, ln2_g, ln2_b, w_router, w_gate, w_up, w_down, ln3_g, ln3_b):
    row = lambda v: v.reshape(1, -1).astype(F32)
    w = w_in[0]
    o = np.cumsum((0, NA_WIDTH, NA_WIDTH, NA_WIDTH, GLA_KEY_WIDTH, GLA_KEY_WIDTH, GLA_WIDTH, GLA_WIDTH,
                   2 * GLA_GATE_RANK))
    wgk = jnp.zeros((D_MODEL, LANES), F32).at[:, :2 * GLA_GATE_RANK].set(w[:, o[7]:o[8]])
    w2 = jnp.zeros((LANES, 2 * GLA_KEY_WIDTH), F32)
    for s in range(2):
        w2 = w2.at[s * GLA_GATE_RANK:(s + 1) * GLA_GATE_RANK,
                   s * GLA_KEY_WIDTH:(s + 1) * GLA_KEY_WIDTH].set(gla_gate_w2[0, s])
    tab, rmask = _na_tables(na_rpb[0])
    wr_t = w_router[0].T.astype(F32)
    wr_hi = wr_t.astype(BF16)
    wr_lo = (wr_t - wr_hi.astype(F32)).astype(BF16)
    return dict(
        ln_in=(row(ln_in_g), row(ln_in_b)),
        wna=w[:, o[0]:o[3]].astype(BF16), wqk=w[:, o[3]:o[5]].astype(BF16), wv=w[:, o[5]:o[6]].astype(BF16),
        wr=w[:, o[6]:o[7]].astype(BF16), wgk=wgk.astype(BF16), w2=w2.astype(BF16),
        gb=gla_gate_b[0].reshape(1, -1).astype(F32),
        tab=tab, rmask=rmask, ng=row(gla_norm_g[0]),
        wo1=w_out[0][:NA_WIDTH].astype(BF16), wo2=w_out[0][NA_WIDTH:].astype(BF16),
        ln1=(row(ln1_g[0]), row(ln1_b[0])),
        wq=mem_wq[0].astype(BF16), wkv=mem_wkv[0].astype(BF16), wmo=mem_wo[0].astype(BF16),
        ln2=(row(ln2_g[0]), row(ln2_b[0])),
        wr_hi=wr_hi, wr_lo=wr_lo,
        wg=w_gate[0].astype(BF16), wu=w_up[0].astype(BF16), wd=w_down[0].astype(BF16),
        ln3=(row(ln3_g[0]), row(ln3_b[0])),
    )


def _trunk(x, mem, p):
    batch, seq, _ = x.shape
    n = batch * seq
    cap = EC_CAPACITY_FACTOR * n // N_EXPERTS
    h0, qkv, qk, v, r, la = _in_proj(x.reshape(n, D_MODEL), *p["ln_in"], p["wna"], p["wqk"], p["wv"], p["wr"],
                                     p["wgk"], p["w2"], p["gb"])
    na = _na(qkv, p["tab"], p["rmask"], batch, seq)
    of, ob = _gla(qk, v, la, batch, seq)
    h1 = _mix_out(na, of, ob, r, p["ng"], h0, p["wo1"], p["wo2"], *p["ln1"])
    kv = _mem_kv(mem.reshape(batch * MEM_TOKENS, D_MODEL), p["wkv"])
    h2, h2t, aff_t = _xattn(h1, kv, p["wq"], p["wmo"], *p["ln2"], p["wr_hi"], p["wr_lo"], batch, seq)
    idx, gate, pos, bst = _route(aff_t, cap)
    ye = _ffn(idx, gate, p["wg"], p["wu"], p["wd"], h2t, cap)
    bst = jnp.concatenate([bst[:, 0, :], jnp.full((N_EXPERTS, 1), cap, I32)], axis=1)
    out = _combine(bst, pos.reshape(N_EXPERTS, n), h2, *p["ln3"], ye, cap)
    return out.reshape(batch, seq, D_MODEL)


def kernel(x_prompt, x_sample, mem_prompt, mem_sample, ln_in_g, ln_in_b, w_in, na_rpb, gla_gate_w2, gla_gate_b,
           gla_norm_g, w_out, ln1_g, ln1_b, mem_wq, mem_wkv, mem_wo, ln2_g, ln2_b, w_router, w_gate, w_up, w_down,
           ln3_g, ln3_b):
    p = _prepare(ln_in_g, ln_in_b, w_in, na_rpb, gla_gate_w2, gla_gate_b, gla_norm_g, w_out, ln1_g, ln1_b,
                 mem_wq, mem_wkv, mem_wo, ln2_g, ln2_b, w_router, w_gate, w_up, w_down, ln3_g, ln3_b)
    return _trunk(x_prompt, mem_prompt, p), _trunk(x_sample, mem_sample, p)
```

```python
import functools

import numpy as np
import jax
import jax.numpy as jnp
from jax import lax
from jax.experimental import pallas as pl
from jax.experimental.pallas import tpu as pltpu

F32 = jnp.float32
BF16 = jnp.bfloat16
I32 = jnp.int32

D_MODEL = 1024
GRID_W = 64
NA_HEADS = 8
NA_HEAD_DIM = 64
NA_WIDTH = NA_HEADS * NA_HEAD_DIM
NA_WIN_ROWS = 8
NA_WIN_COLS = 16
GLA_HEADS = 4
GLA_DK = 64
GLA_DV = 128
GLA_KEY_WIDTH = GLA_HEADS * GLA_DK
GLA_WIDTH = GLA_HEADS * GLA_DV
GLA_GATE_RANK = 16
GLA_GATE_NORM = 16.0
GLA_CHUNK = 64
MEM_TOKENS = 256
MEM_HEADS = 4
MEM_HEAD_DIM = D_MODEL // MEM_HEADS
N_EXPERTS = 16
EC_CAPACITY_FACTOR = 2
D_FF = 2 * D_MODEL
LN_EPS = 1e-5
RMS_EPS = 1e-5
DEPTH = 1
DEEPNORM_ALPHA = (2 * DEPTH) ** 0.25

LANES = 128
SUBLANES = 8
NEG = -1e30
VMEM_LIMIT = 56 * 1024 * 1024

NA_QROWS = 8
NA_SLAB = 16
GLA_SUPER = 512
GLA_SEQS = 2
TOK_TILE = 512
FFN_ROWS = 512
CMB_TILE = 256
CMB_WIN = 64


def _dot(a, b):
    return jnp.dot(a, b, preferred_element_type=F32)


def _dot_nt(a, b):
    return lax.dot_general(a, b, (((1,), (1,)), ((), ())), preferred_element_type=F32)


def _dot_tn(a, b):
    return lax.dot_general(a, b, (((0,), (0,)), ((), ())), preferred_element_type=F32)


def _layer_norm(x, g, b):
    mu = jnp.mean(x, axis=-1, keepdims=True)
    xc = x - mu
    var = jnp.mean(xc * xc, axis=-1, keepdims=True)
    return xc * lax.rsqrt(var + LN_EPS) * g + b


def _split3(x):
    hi = x.astype(BF16)
    r1 = x - hi.astype(F32)
    mid = r1.astype(BF16)
    lo = (r1 - mid.astype(F32)).astype(BF16)
    return hi, mid, lo


def _params(*sem):
    return pltpu.CompilerParams(dimension_semantics=sem, vmem_limit_bytes=VMEM_LIMIT)


def _const_spec(shape):
    nd = len(shape)
    return pl.BlockSpec(shape, lambda *_: (0,) * nd)


def _in_proj_kernel(x_ref, g_ref, b_ref, wna_ref, wqk_ref, wv_ref, wr_ref, wgk_ref, w2_ref, gb_ref,
                    h0_ref, qkv_ref, qk_ref, v_ref, r_ref, la_ref):
    h0 = _layer_norm(x_ref[...], g_ref[...], b_ref[...])
    h0_ref[...] = h0
    hb = h0.astype(BF16)
    qkv_ref[...] = _dot(hb, wna_ref[...]).astype(BF16)
    qk_ref[...] = _dot(hb, wqk_ref[...])
    v_ref[...] = _dot(hb, wv_ref[...]).astype(BF16)
    r_ref[...] = _dot(hb, wr_ref[...])
    gk = _dot(hb, wgk_ref[...])
    z = _dot(gk.astype(BF16), w2_ref[...]) + gb_ref[...]
    log_sig = jnp.minimum(z, 0.0) - jnp.log1p(jnp.exp(-jnp.abs(z)))
    la_ref[...] = log_sig * (1.0 / GLA_GATE_NORM)


def _in_proj(x, g, b, wna, wqk, wv, wr, wgk, w2, gb):
    n = x.shape[0]
    tm = TOK_TILE
    row = lambda w: pl.BlockSpec((tm, w), lambda i: (i, 0))
    outs = (
        jax.ShapeDtypeStruct((n, D_MODEL), F32),
        jax.ShapeDtypeStruct((n, 3 * NA_WIDTH), BF16),
        jax.ShapeDtypeStruct((n, 2 * GLA_KEY_WIDTH), F32),
        jax.ShapeDtypeStruct((n, GLA_WIDTH), BF16),
        jax.ShapeDtypeStruct((n, GLA_WIDTH), F32),
        jax.ShapeDtypeStruct((n, 2 * GLA_KEY_WIDTH), F32),
    )
    return pl.pallas_call(
        _in_proj_kernel,
        grid=(n // tm,),
        in_specs=[row(D_MODEL), _const_spec(g.shape), _const_spec(b.shape), _const_spec(wna.shape),
                  _const_spec(wqk.shape), _const_spec(wv.shape), _const_spec(wr.shape),
                  _const_spec(wgk.shape), _const_spec(w2.shape), _const_spec(gb.shape)],
        out_specs=(row(D_MODEL), row(3 * NA_WIDTH), row(2 * GLA_KEY_WIDTH), row(GLA_WIDTH), row(GLA_WIDTH),
                   row(2 * GLA_KEY_WIDTH)),
        out_shape=outs,
        name="ln_in_proj",
        compiler_params=_params("parallel"),
    )(x, g, b, wna, wqk, wv, wr, wgk, w2, gb)


def _na_tables(rpb):
    kw = NA_WIN_COLS
    cols = np.arange(GRID_W)
    col_start = np.clip(cols - kw // 2, 0, GRID_W - kw)
    j = np.arange(GRID_W)
    col_ok = (j[None, :] >= col_start[:, None]) & (j[None, :] < col_start[:, None] + kw)
    col_off = np.clip(j[None, :] - cols[:, None] + (NA_WIN_COLS - 1), 0, 2 * NA_WIN_COLS - 2)
    nblk = NA_SLAB + NA_QROWS
    ro = np.arange(nblk) - 5
    ro_ok = (ro >= 0) & (ro < 2 * NA_WIN_ROWS - 1)
    ro_c = np.clip(ro, 0, 2 * NA_WIN_ROWS - 2)
    t = rpb[:, ro_c][:, :, col_off]
    ok = jnp.asarray(ro_ok[None, :, None, None] & col_ok[None, None])
    t = jnp.where(ok, t, NEG).astype(F32)
    flat = jnp.transpose(t, (0, 2, 1, 3)).reshape(NA_HEADS, GRID_W, nblk * GRID_W)
    shifted = jnp.concatenate([flat[:, :, GRID_W:], jnp.full((NA_HEADS, GRID_W, GRID_W), NEG, F32)], axis=-1)
    tab = jnp.stack([flat, shifted])

    a = np.arange(NA_QROWS)
    i = np.arange(NA_SLAB)
    masks = []
    for w in (np.maximum(a, 4), a, np.minimum(a, 4)):
        valid = (i[None, :] >= w[:, None]) & (i[None, :] < w[:, None] + NA_WIN_ROWS)
        m = np.where(valid, 0.0, NEG).astype(np.float32)
        masks.append(np.repeat(np.repeat(m, GRID_W, axis=0), GRID_W, axis=1))
    return tab, jnp.asarray(np.stack(masks))


def _na_kernel(q_ref, kp_ref, kc_ref, kn_ref, vp_ref, vc_ref, vn_ref, tab_ref, rm_ref, o_ref):
    half = NA_QROWS * GRID_W // 2
    k = jnp.concatenate([kp_ref[half:, :], kc_ref[...], kn_ref[:half, :]], axis=0)
    v = jnp.concatenate([vp_ref[half:, :], vc_ref[...], vn_ref[:half, :]], axis=0)
    rm = rm_ref[0]
    nq = NA_QROWS * GRID_W
    nk = NA_SLAB * GRID_W
    lane = lax.broadcasted_iota(I32, (nq, LANES), 1)
    first = lane < NA_HEAD_DIM
    for hp in range(NA_HEADS // 2):
        sl = slice(hp * LANES, (hp + 1) * LANES)
        q2 = q_ref[:, sl]
        k2 = k[:, sl]
        v2 = v[:, sl]
        pv = []
        for sub in range(2):
            h = 2 * hp + sub
            qm = jnp.where(first if sub == 0 else jnp.logical_not(first), q2, jnp.zeros_like(q2))
            s = _dot_nt(qm, k2) * (NA_HEAD_DIM ** -0.5)
            strips = []
            for a in range(NA_QROWS):
                par = a % 2
                off = ((8 - a) if par == 0 else (7 - a)) * GRID_W
                strips.append(s[a * GRID_W:(a + 1) * GRID_W, :] + tab_ref[par, h, :, off:off + nk])
            s = jnp.concatenate(strips, axis=0) + rm
            m = jnp.max(s, axis=-1, keepdims=True)
            e = jnp.exp(s - m)
            l = jnp.sum(e, axis=-1, keepdims=True)
            pv.append(_dot(e.astype(BF16), v2) / l)
        o_ref[:, sl] = jnp.where(first, pv[0], pv[1]).astype(BF16)


def _na(qkv, tab, rmask, batch, seq):
    n = batch * seq
    nq = NA_QROWS * GRID_W
    nblk = seq // nq
    assert seq % nq == 0 and nblk >= 2
    cur = lambda c: pl.BlockSpec((nq, NA_WIDTH), lambda b, r: (b * nblk + r, c))
    prv = lambda c: pl.BlockSpec((nq, NA_WIDTH), lambda b, r: (b * nblk + jnp.maximum(r - 1, 0), c))
    nxt = lambda c: pl.BlockSpec((nq, NA_WIDTH), lambda b, r: (b * nblk + jnp.minimum(r + 1, nblk - 1), c))
    variant = lambda b, r: (jnp.where(r == 0, 0, jnp.where(r == nblk - 1, 2, 1)), 0, 0)
    return pl.pallas_call(
        _na_kernel,
        grid=(batch, nblk),
        in_specs=[cur(0), prv(1), cur(1), nxt(1), prv(2), cur(2), nxt(2),
                  _const_spec(tab.shape), pl.BlockSpec((1,) + rmask.shape[1:], variant)],
        out_specs=pl.BlockSpec((nq, NA_WIDTH), lambda b, r: (b * nblk + r, 0)),
        out_shape=jax.ShapeDtypeStruct((n, NA_WIDTH), BF16),
        name="nbr_attn",
        compiler_params=_params("parallel", "parallel"),
    )(qkv, qkv, qkv, qkv, qkv, qkv, qkv, tab, rmask)


def _gla_kernel(qkf_ref, vf_ref, laf_ref, qkb_ref, vb_ref, lab_ref, of_ref, ob_ref, sf_ref, sb_ref):
    c = GLA_CHUNK
    kwid = GLA_KEY_WIDTH
    nchunk = GLA_SUPER // c

    @pl.when(pl.program_id(1) == 0)
    def _():
        sf_ref[...] = jnp.zeros_like(sf_ref)
        sb_ref[...] = jnp.zeros_like(sb_ref)

    ri = lax.broadcasted_iota(I32, (c, c), 0)
    ci = lax.broadcasted_iota(I32, (c, c), 1)
    tri_f = (ci <= ri).astype(BF16)
    tri_b = (ci >= ri).astype(BF16)
    ar = lax.broadcasted_iota(I32, (c, kwid), 0)
    ac = lax.broadcasted_iota(I32, (c, kwid), 1) % c
    keep_f = ac <= ar
    keep_b = ac >= ar
    wr_ = lax.broadcasted_iota(I32, (kwid, kwid), 0) // c
    wc_ = lax.broadcasted_iota(I32, (kwid, kwid), 1) // GLA_DK
    wmask = wr_ == wc_
    vr_ = lax.broadcasted_iota(I32, (kwid, GLA_WIDTH), 0) // c
    vc_ = lax.broadcasted_iota(I32, (kwid, GLA_WIDTH), 1) // GLA_DV
    vmask = vr_ == vc_
    sr_ = lax.broadcasted_iota(I32, (GLA_WIDTH, kwid), 0) // GLA_DV
    sc_ = lax.broadcasted_iota(I32, (GLA_WIDTH, kwid), 1) // GLA_DK
    smask = sr_ == sc_

    def unit(qk, v, la, tri, keep, last, st_ref):
        q = qk[:, :kwid] * (GLA_DK ** -0.5)
        k = qk[:, kwid:]
        hi, mid, lo = _split3(la)
        b = _dot(tri, hi) + _dot(tri, mid) + _dot(tri, lo)
        bl = b[last:last + 1, :]
        qe = (q * jnp.exp(b)).astype(BF16)
        ke = (k * jnp.exp(-b)).astype(BF16)
        kd = (k * jnp.exp(bl - b)).astype(BF16)
        dec = jnp.exp(bl)
        wt = jnp.where(wmask, jnp.concatenate([ke] * GLA_HEADS, axis=0), jnp.zeros((kwid, kwid), BF16))
        a = jnp.where(keep, _dot_nt(qe, wt), 0.0)
        vbd = jnp.where(vmask, jnp.concatenate([v] * GLA_HEADS, axis=0), jnp.zeros((kwid, GLA_WIDTH), BF16))
        st = st_ref[...]
        o = _dot(a.astype(BF16), vbd) + _dot_nt(qe, st.astype(BF16))
        upd = _dot_tn(v, kd)
        st_ref[...] = st * dec + jnp.where(smask, upd, 0.0)
        return o

    def body(ic, carry):
        rf = pl.ds(pl.multiple_of(ic * c, c), c)
        rb = pl.ds(pl.multiple_of((nchunk - 1 - ic) * c, c), c)
        for j in range(GLA_SEQS):
            of_ref[j, rf, :] = unit(qkf_ref[j, rf, :], vf_ref[j, rf, :], laf_ref[j, rf, :],
                                    tri_f, keep_f, c - 1, sf_ref.at[j])
            ob_ref[j, rb, :] = unit(qkb_ref[j, rb, :], vb_ref[j, rb, :], lab_ref[j, rb, :],
                                    tri_b, keep_b, 0, sb_ref.at[j])
        return carry

    lax.fori_loop(0, nchunk, body, 0)


def _gla(qk, v, la, batch, seq):
    n = batch * seq
    ns = seq // GLA_SUPER
    g = GLA_SEQS
    assert seq % GLA_SUPER == 0 and batch % g == 0
    fwd = lambda w, c: pl.BlockSpec((g, GLA_SUPER, w), lambda b, s: (b, s, c))
    bwd = lambda w, c: pl.BlockSpec((g, GLA_SUPER, w), lambda b, s: (b, ns - 1 - s, c))
    kw2 = 2 * GLA_KEY_WIDTH
    seqs = lambda t: t.reshape(batch, seq, t.shape[-1])
    state = pltpu.VMEM((g, GLA_WIDTH, GLA_KEY_WIDTH), F32)
    of, ob = pl.pallas_call(
        _gla_kernel,
        grid=(batch // g, ns),
        in_specs=[fwd(kw2, 0), fwd(GLA_WIDTH, 0), fwd(GLA_KEY_WIDTH, 0),
                  bwd(kw2, 0), bwd(GLA_WIDTH, 0), bwd(GLA_KEY_WIDTH, 1)],
        out_specs=(fwd(GLA_WIDTH, 0), bwd(GLA_WIDTH, 0)),
        out_shape=(jax.ShapeDtypeStruct((batch, seq, GLA_WIDTH), F32),
                   jax.ShapeDtypeStruct((batch, seq, GLA_WIDTH), F32)),
        scratch_shapes=[state, state],
        name="gla",
        compiler_params=_params("parallel", "arbitrary"),
    )(seqs(qk), seqs(v), seqs(la), seqs(qk), seqs(v), seqs(la))
    return of.reshape(n, GLA_WIDTH), ob.reshape(n, GLA_WIDTH)


def _mix_out_kernel(na_ref, of_ref, ob_ref, r_ref, ng_ref, h0_ref, w1_ref, w2_ref, g_ref, b_ref, h1_ref):
    o = of_ref[...] + ob_ref[...]
    parts = []
    for h in range(GLA_HEADS):
        oh = o[:, h * GLA_DV:(h + 1) * GLA_DV]
        ms = jnp.mean(oh * oh, axis=-1, keepdims=True)
        parts.append(oh * lax.rsqrt(ms + RMS_EPS) * ng_ref[...])
    r = r_ref[...]
    gl = jnp.concatenate(parts, axis=-1) * (r * jax.nn.sigmoid(r))
    mixed = _dot(na_ref[...], w1_ref[...]) + _dot(gl.astype(BF16), w2_ref[...])
    h1_ref[...] = _layer_norm(DEEPNORM_ALPHA * h0_ref[...] + mixed, g_ref[...], b_ref[...])


def _mix_out(na, of, ob, r, ng, h0, w1, w2, g, b):
    n = na.shape[0]
    tm = TOK_TILE
    row = lambda w: pl.BlockSpec((tm, w), lambda i: (i, 0))
    return pl.pallas_call(
        _mix_out_kernel,
        grid=(n // tm,),
        in_specs=[row(NA_WIDTH), row(GLA_WIDTH), row(GLA_WIDTH), row(GLA_WIDTH), _const_spec(ng.shape),
                  row(D_MODEL), _const_spec(w1.shape), _const_spec(w2.shape), _const_spec(g.shape),
                  _const_spec(b.shape)],
        out_specs=row(D_MODEL),
        out_shape=jax.ShapeDtypeStruct((n, D_MODEL), F32),
        name="mix_out_ln1",
        compiler_params=_params("parallel"),
    )(na, of, ob, r, ng, h0, w1, w2, g, b)


def _mem_kv_kernel(m_ref, w_ref, o_ref):
    o_ref[...] = _dot(m_ref[...].astype(BF16), w_ref[...]).astype(BF16)


def _mem_kv(mem, wkv):
    rows = mem.shape[0]
    tn = D_MODEL
    return pl.pallas_call(
        _mem_kv_kernel,
        grid=(rows // MEM_TOKENS, wkv.shape[1] // tn),
        in_specs=[pl.BlockSpec((MEM_TOKENS, D_MODEL), lambda i, j: (i, 0)),
                  pl.BlockSpec((D_MODEL, tn), lambda i, j: (0, j))],
        out_specs=pl.BlockSpec((MEM_TOKENS, tn), lambda i, j: (i, j)),
        out_shape=jax.ShapeDtypeStruct((rows, wkv.shape[1]), BF16),
        name="mem_kv",
        compiler_params=_params("parallel", "parallel"),
    )(mem, wkv)


def _xattn_kernel(h1_ref, kv_ref, wq_ref, wo_ref, g_ref, b_ref, wrh_ref, wrl_ref, h2_ref, h2t_ref, aff_ref):
    h1 = h1_ref[...]
    q = _dot(h1.astype(BF16), wq_ref[...]).astype(BF16)
    outs = []
    for h in range(MEM_HEADS):
        sl = slice(h * MEM_HEAD_DIM, (h + 1) * MEM_HEAD_DIM)
        kh = kv_ref[:, sl]
        vh = kv_ref[:, D_MODEL + h * MEM_HEAD_DIM:D_MODEL + (h + 1) * MEM_HEAD_DIM]
        s = _dot_nt(q[:, sl], kh) * (MEM_HEAD_DIM ** -0.5)
        m = jnp.max(s, axis=-1, keepdims=True)
        e = jnp.exp(s - m)
        p = (e / jnp.sum(e, axis=-1, keepdims=True)).astype(BF16)
        outs.append(_dot(p, vh))
    o = jnp.concatenate(outs, axis=-1).astype(BF16)
    h2 = _layer_norm(DEEPNORM_ALPHA * h1 + _dot(o, wo_ref[...]), g_ref[...], b_ref[...])
    h2_ref[...] = h2
    for c in range(D_MODEL // LANES):
        h2t_ref[:, c, :] = h2[:, c * LANES:(c + 1) * LANES]
    hh = h2.astype(BF16)
    hl = (h2 - hh.astype(F32)).astype(BF16)
    logits = _dot_nt(wrh_ref[...], hh) + _dot_nt(wrh_ref[...], hl) + _dot_nt(wrl_ref[...], hh)
    m = jnp.max(logits, axis=0, keepdims=True)
    e = jnp.exp(logits - m)
    aff_ref[...] = e / jnp.sum(e, axis=0, keepdims=True)


def _xattn(h1, kv, wq, wo, g, b, wrh, wrl, batch, seq):
    n = batch * seq
    tm = TOK_TILE
    nt = seq // tm
    row = pl.BlockSpec((tm, D_MODEL), lambda i: (i, 0))
    return pl.pallas_call(
        _xattn_kernel,
        grid=(n // tm,),
        in_specs=[row, pl.BlockSpec((MEM_TOKENS, 2 * D_MODEL), lambda i: (i // nt, 0)),
                  _const_spec(wq.shape), _const_spec(wo.shape), _const_spec(g.shape), _const_spec(b.shape),
                  _const_spec(wrh.shape), _const_spec(wrl.shape)],
        out_specs=(row, pl.BlockSpec((tm, D_MODEL // LANES, LANES), lambda i: (i, 0, 0)),
                   pl.BlockSpec((N_EXPERTS, tm), lambda i: (0, i))),
        out_shape=(jax.ShapeDtypeStruct((n, D_MODEL), F32),
                   jax.ShapeDtypeStruct((n, D_MODEL // LANES, LANES), F32),
                   jax.ShapeDtypeStruct((N_EXPERTS, n), F32)),
        name="xattn_ln2_router",
        compiler_params=_params("parallel"),
    )(h1, kv, wq, wo, g, b, wrh, wrl)


def _route_kernel(aff_ref, idx_ref, gate_ref, pos_ref, bst_ref, *, cap, pchunk):
    a = aff_ref[0]
    nb = a.shape[0]
    bits = lax.bitcast_convert_type(a, I32)

    def total(x):
        return jnp.sum(jnp.sum(x, axis=0, keepdims=True), axis=1, keepdims=True)

    def search(_, carry):
        lo, hi = carry
        mid = lo + ((hi - lo + 1) >> 1)
        ok = total((bits >= mid).astype(F32)) >= cap
        return jnp.where(ok, mid, lo), jnp.where(ok, hi, mid - 1)

    lo0 = jnp.zeros((1, 1), I32)
    hi0 = jnp.full((1, 1), 0x7F800000, I32)
    thr, _ = lax.fori_loop(0, 31, search, (lo0, hi0))

    li = lax.broadcasted_iota(I32, (LANES, LANES), 0)
    lj = lax.broadcasted_iota(I32, (LANES, LANES), 1)
    upper = (li <= lj).astype(BF16)
    bi = lax.broadcasted_iota(I32, (nb, nb), 0)
    bj = lax.broadcasted_iota(I32, (nb, nb), 1)
    strict = (bj < bi).astype(BF16)
    incl = (bi <= bj).astype(BF16)
    ones8 = jnp.ones((8, LANES), BF16)

    def prefix(mask_b):
        rowcum = _dot(mask_b, upper)
        rowtot = jnp.broadcast_to(rowcum[:, LANES - 1:LANES], (nb, LANES)).astype(BF16)
        return rowcum + _dot(strict, rowtot)

    gt = bits > thr
    eq = bits == thr
    need = cap - total(gt.astype(F32))
    sel = jnp.logical_or(gt, jnp.logical_and(eq, prefix(eq.astype(BF16)) <= need))
    selb = sel.astype(BF16)
    cnt = prefix(selb)

    rt_row = _dot_nt(ones8, selb)
    bend = _dot(rt_row.astype(BF16), incl)
    bstart = bend - rt_row
    pos_ref[0] = jnp.where(sel, cnt - 1.0, -1.0).astype(I32)
    bst_ref[0] = bstart[0:1, :].astype(I32)

    cnt_hi = jnp.floor(cnt * (1.0 / 64.0))
    cnt_lo = cnt - 64.0 * cnt_hi
    a_hi, a_mid, a_lo = _split3(a)
    rowid = lax.broadcasted_iota(I32, (nb, LANES), 0).astype(BF16)
    rhs = jnp.concatenate([cnt_hi.astype(BF16), cnt_lo.astype(BF16), selb, rowid, a_hi, a_mid, a_lo], axis=1)
    lanef = lax.broadcasted_iota(I32, (pchunk, LANES), 1).astype(F32)

    for pc in range(cap // pchunk):
        p = (lax.broadcasted_iota(I32, (pchunk, nb), 0) + pc * pchunk).astype(F32)
        g = jnp.logical_and(bstart[0:1, :] <= p, p < bend[0:1, :]).astype(BF16)
        rows = _dot(g, rhs)
        part = lambda k: rows[:, k * LANES:(k + 1) * LANES]
        crow = part(0) * 64.0 + part(1)
        p1 = (lax.broadcasted_iota(I32, (pchunk, LANES), 0) + (pc * pchunk + 1)).astype(F32)
        oh = jnp.logical_and(crow == p1, part(2) > 0.5)
        pick = lambda x: jnp.where(oh, x, 0.0)
        tok_row = _dot_nt(ones8, pick(part(3)).astype(BF16))
        tok_lane = _dot_nt(ones8, pick(lanef).astype(BF16))
        idx_ref[0, :, pc * pchunk:(pc + 1) * pchunk] = (tok_row[0:1] * float(LANES) + tok_lane[0:1]).astype(I32)
        gate = jnp.sum(pick(part(4) + part(5) + part(6)), axis=1, keepdims=True)
        gate_ref[pc * pchunk:(pc + 1) * pchunk, :] = jnp.broadcast_to(gate, (pchunk, LANES))


def _route(aff_t, cap):
    e, n = aff_t.shape
    nb = n // LANES
    assert nb <= 256 and cap % 64 == 0 and cap // 64 <= 256
    pchunk = min(1024, cap)
    aff3 = aff_t.reshape(e, nb, LANES)
    return pl.pallas_call(
        functools.partial(_route_kernel, cap=cap, pchunk=pchunk),
        grid=(e,),
        in_specs=[pl.BlockSpec((1, nb, LANES), lambda i: (i, 0, 0))],
        out_specs=(pl.BlockSpec((1, 1, cap), lambda i: (i, 0, 0)),
                   pl.BlockSpec((cap, LANES), lambda i: (i, 0)),
                   pl.BlockSpec((1, nb, LANES), lambda i: (i, 0, 0)),
                   pl.BlockSpec((1, 1, nb), lambda i: (i, 0, 0))),
        out_shape=(jax.ShapeDtypeStruct((e, 1, cap), I32), jax.ShapeDtypeStruct((e * cap, LANES), F32),
                   jax.ShapeDtypeStruct((e, nb, LANES), I32), jax.ShapeDtypeStruct((e, 1, nb), I32)),
        name="route",
        compiler_params=_params("parallel"),
    )(aff3)


def _ffn_kernel(idx_ref, nxt_ref, gate_ref, wg_ref, wu_ref, wd_ref, h2_hbm, ye_ref, xbuf, sem):
    s = pl.program_id(0)
    slot = s % 2

    def row_copy(rows_ref, dst_slot, i):
        return pltpu.make_async_copy(h2_hbm.at[rows_ref[0, 0, i]], xbuf.at[dst_slot, i], sem.at[dst_slot])

    def wait_slot(dst_slot):
        pltpu.make_async_copy(h2_hbm.at[pl.ds(0, FFN_ROWS)], xbuf.at[dst_slot], sem.at[dst_slot]).wait()

    @pl.when(s == 0)
    def _():
        def body(i, carry):
            row_copy(idx_ref, 0, i).start()
            return carry
        lax.fori_loop(0, FFN_ROWS, body, 0, unroll=8)

    wait_slot(slot)
    x = jnp.concatenate([xbuf[slot, :, c, :] for c in range(D_MODEL // LANES)], axis=-1).astype(BF16)
    for i in range(FFN_ROWS):
        row_copy(nxt_ref, 1 - slot, i).start(priority=i % 2)
    a = _dot(x, wg_ref[0])
    u = _dot(x, wu_ref[0])
    hmid = (a * jax.nn.sigmoid(a) * u).astype(BF16)
    ye_ref[...] = _dot(hmid, wd_ref[0]) * gate_ref[:, 0:1]

    @pl.when(s + 1 == pl.num_programs(0))
    def _():
        wait_slot(1 - slot)


def _ffn(idx, gate, wg, wu, wd, h2, cap):
    e = idx.shape[0]
    tr = FFN_ROWS
    per = cap // tr
    nsteps = e * per
    assert cap % tr == 0
    wspec = lambda shp: pl.BlockSpec((1,) + shp, lambda s: (s // per, 0, 0))
    rows = lambda shift: pl.BlockSpec(
        (1, 1, tr), lambda s: (jnp.minimum(s + shift, nsteps - 1) // per, 0, jnp.minimum(s + shift, nsteps - 1) % per),
        memory_space=pltpu.SMEM)
    return pl.pallas_call(
        _ffn_kernel,
        grid=(nsteps,),
        in_specs=[rows(0), rows(1), pl.BlockSpec((tr, LANES), lambda s: (s, 0)),
                  wspec((D_MODEL, D_FF)), wspec((D_MODEL, D_FF)), wspec((D_FF, D_MODEL)),
                  pl.BlockSpec(memory_space=pl.ANY)],
        out_specs=pl.BlockSpec((tr, D_MODEL), lambda s: (s, 0)),
        out_shape=jax.ShapeDtypeStruct((e * cap, D_MODEL), F32),
        scratch_shapes=[pltpu.VMEM((2, tr, D_MODEL // LANES, LANES), F32), pltpu.SemaphoreType.DMA((2,))],
        name="ffn",
        compiler_params=_params("arbitrary"),
    )(idx, idx, gate, wg, wu, wd, h2)


def _combine_kernel(bst_ref, pos_ref, h2_ref, g_ref, b_ref, ye_hbm, o_ref, ybuf, sem, *, cap):
    e_n = N_EXPERTS
    w = CMB_WIN
    step = w - SUBLANES
    per = CMB_TILE // LANES
    t = pl.program_id(0)
    last = pl.num_programs(0) - 1
    slot = t % 2
    wi = lax.broadcasted_iota(I32, (w, CMB_TILE), 0)

    def windows(tile, r):
        lo = [bst_ref[e, tile * per] + r * step for e in range(e_n)]
        src = [pl.multiple_of(jnp.minimum(e * cap + (lo[e] // SUBLANES) * SUBLANES, e_n * cap - w), SUBLANES)
               for e in range(e_n)]
        return lo, src

    def copies(src, dst_slot):
        return [pltpu.make_async_copy(ye_hbm.at[pl.ds(src[e], w)], ybuf.at[dst_slot, pl.ds(e * w, w)],
                                      sem.at[dst_slot, e]) for e in range(e_n)]

    def contribution(lo, src, buf_slot):
        onehot = []
        for e in range(e_n):
            base = src[e] - e * cap
            p = pos_ref[e:e + 1, :]
            this_round = jnp.logical_and(p >= lo[e], p < lo[e] + step)
            onehot.append(jnp.logical_and(p == base + wi, this_round).astype(BF16))
        oh_t = jnp.concatenate(onehot, axis=0)
        yv = ybuf[buf_slot]
        hi = yv.astype(BF16)
        lo_part = (yv - hi.astype(F32)).astype(BF16)
        return _dot_tn(oh_t, hi) + _dot_tn(oh_t, lo_part)

    lo0, src0 = windows(t, 0)

    @pl.when(t == 0)
    def _():
        for c in copies(src0, 0):
            c.start()

    _, src_next = windows(jnp.minimum(t + 1, last), 0)
    for c in copies(src_next, 1 - slot):
        c.start()
    for c in copies(src0, slot):
        c.wait()
    y = contribution(lo0, src0, slot)

    counts = [bst_ref[e, (t + 1) * per] - bst_ref[e, t * per] for e in range(e_n)]
    nrounds = functools.reduce(jnp.maximum, [(c + (step - 1)) // step for c in counts])

    def extra_round(r, y):
        lo, src = windows(t, r)
        for c in copies(src, slot):
            c.start()
        for c in copies(src, slot):
            c.wait()
        return y + contribution(lo, src, slot)

    y = lax.fori_loop(1, nrounds, extra_round, y)
    o_ref[...] = _layer_norm(DEEPNORM_ALPHA * h2_ref[...] + y, g_ref[...], b_ref[...])

    @pl.when(t == last)
    def _():
        for c in copies(src_next, 1 - slot):
            c.wait()


def _combine(bst, pos, h2, g, b, ye, cap):
    n = h2.shape[0]
    tt = CMB_TILE
    assert cap >= CMB_WIN and n % tt == 0
    grid_spec = pltpu.PrefetchScalarGridSpec(
        num_scalar_prefetch=1,
        grid=(n // tt,),
        in_specs=[pl.BlockSpec((N_EXPERTS, tt), lambda i, *_: (0, i)),
                  pl.BlockSpec((tt, D_MODEL), lambda i, *_: (i, 0)),
                  pl.BlockSpec(g.shape, lambda i, *_: (0, 0)), pl.BlockSpec(b.shape, lambda i, *_: (0, 0)),
                  pl.BlockSpec(memory_space=pl.ANY)],
        out_specs=pl.BlockSpec((tt, D_MODEL), lambda i, *_: (i, 0)),
        scratch_shapes=[pltpu.VMEM((2, N_EXPERTS * CMB_WIN, D_MODEL), F32),
                        pltpu.SemaphoreType.DMA((2, N_EXPERTS))],
    )
    return pl.pallas_call(
        functools.partial(_combine_kernel, cap=cap),
        grid_spec=grid_spec,
        out_shape=jax.ShapeDtypeStruct((n, D_MODEL), F32),
        name="combine_ln3",
        compiler_params=_params("arbitrary"),
    )(bst, pos, h2, g, b, ye)


def _prepare(ln_in_g, ln_in_b, w_in, na_rpb, gla_gate_w2, gla_gate_b, gla_norm_g, w_out, ln1_g, ln1_b,
             mem_wq, mem_wkv, mem_wo, ln2_g, ln2_b, w_router, w_gate, w_up, w_down, ln3_g, ln3_b):
    row = lambda v: v.reshape(1, -1).astype(F32)
    w = w_in[0]
    o = np.cumsum((0, NA_WIDTH, NA_WIDTH, NA_WIDTH, GLA_KEY_WIDTH, GLA_KEY_WIDTH, GLA_WIDTH, GLA_WIDTH,
                   2 * GLA_GATE_RANK))
    wgk = jnp.zeros((D_MODEL, LANES), F32).at[:, :2 * GLA_GATE_RANK].set(w[:, o[7]:o[8]])
    w2 = jnp.zeros((LANES, 2 * GLA_KEY_WIDTH), F32)
    for s in range(2):
        w2 = w2.at[s * GLA_GATE_RANK:(s + 1) * GLA_GATE_RANK,
                   s * GLA_KEY_WIDTH:(s + 1) * GLA_KEY_WIDTH].set(gla_gate_w2[0, s])
    tab, rmask = _na_tables(na_rpb[0])
    wr_t = w_router[0].T.astype(F32)
    wr_hi = wr_t.astype(BF16)
    wr_lo = (wr_t - wr_hi.astype(F32)).astype(BF16)
    return dict(
        ln_in=(row(ln_in_g), row(ln_in_b)),
        wna=w[:, o[0]:o[3]].astype(BF16), wqk=w[:, o[3]:o[5]].astype(BF16), wv=w[:, o[5]:o[6]].astype(BF16),
        wr=w[:, o[6]:o[7]].astype(BF16), wgk=wgk.astype(BF16), w2=w2.astype(BF16),
        gb=gla_gate_b[0].reshape(1, -1).astype(F32),
        tab=tab, rmask=rmask, ng=row(gla_norm_g[0]),
        wo1=w_out[0][:NA_WIDTH].astype(BF16), wo2=w_out[0][NA_WIDTH:].astype(BF16),
        ln1=(row(ln1_g[0]), row(ln1_b[0])),
        wq=mem_wq[0].astype(BF16), wkv=mem_wkv[0].astype(BF16), wmo=mem_wo[0].astype(BF16),
        ln2=(row(ln2_g[0]), row(ln2_b[0])),
        wr_hi=wr_hi, wr_lo=wr_lo,
        wg=w_gate[0].astype(BF16), wu=w_up[0].astype(BF16), wd=w_down[0].astype(BF16),
        ln3=(row(ln3_g[0]), row(ln3_b[0])),
    )


def _trunk(x, mem, p):
    batch, seq, _ = x.shape
    n = batch * seq
    cap = EC_CAPACITY_FACTOR * n // N_EXPERTS
    h0, qkv, qk, v, r, la = _in_proj(x.reshape(n, D_MODEL), *p["ln_in"], p["wna"], p["wqk"], p["wv"], p["wr"],
                                     p["wgk"], p["w2"], p["gb"])
    na = _na(qkv, p["tab"], p["rmask"], batch, seq)
    of, ob = _gla(qk, v, la, batch, seq)
    h1 = _mix_out(na, of, ob, r, p["ng"], h0, p["wo1"], p["wo2"], *p["ln1"])
    kv = _mem_kv(mem.reshape(batch * MEM_TOKENS, D_MODEL), p["wkv"])
    h2, h2t, aff_t = _xattn(h1, kv, p["wq"], p["wmo"], *p["ln2"], p["wr_hi"], p["wr_lo"], batch, seq)
    idx, gate, pos, bst = _route(aff_t, cap)
    ye = _ffn(idx, gate, p["wg"], p["wu"], p["wd"], h2t, cap)
    bst = jnp.concatenate([bst[:, 0, :], jnp.full((N_EXPERTS, 1), cap, I32)], axis=1)
    out = _combine(bst, pos.reshape(N_EXPERTS, n), h2, *p["ln3"], ye, cap)
    return out.reshape(batch, seq, D_MODEL)


def kernel(x_prompt, x_sample, mem_prompt, mem_sample, ln_in_g, ln_in_b, w_in, na_rpb, gla_gate_w2, gla_gate_b,
           gla_norm_g, w_out, ln1_g, ln1_b, mem_wq, mem_wkv, mem_wo, ln2_g, ln2_b, w_router, w_gate, w_up, w_down,
           ln3_g, ln3_b):
    p = _prepare(ln_in_g, ln_in_b, w_in, na_rpb, gla_gate_w2, gla_gate_b, gla_norm_g, w_out, ln1_g, ln1_b,
                 mem_wq, mem_wkv, mem_wo, ln2_g, ln2_b, w_router, w_gate, w_up, w_down, ln3_g, ln3_b)
    return _trunk(x_prompt, mem_prompt, p), _trunk(x_sample, mem_sample, p)
```

```python
import functools

import numpy as np
import jax
import jax.numpy as jnp
from jax import lax
from jax.experimental import pallas as pl
from jax.experimental.pallas import tpu as pltpu

F32 = jnp.float32
BF16 = jnp.bfloat16
I32 = jnp.int32

D_MODEL = 1024
GRID_W = 64
NA_HEADS = 8
NA_HEAD_DIM = 64
NA_WIDTH = NA_HEADS * NA_HEAD_DIM
NA_WIN_ROWS = 8
NA_WIN_COLS = 16
GLA_HEADS = 4
GLA_DK = 64
GLA_DV = 128
GLA_KEY_WIDTH = GLA_HEADS * GLA_DK
GLA_WIDTH = GLA_HEADS * GLA_DV
GLA_GATE_RANK = 16
GLA_GATE_NORM = 16.0
GLA_CHUNK = 64
MEM_TOKENS = 256
MEM_HEADS = 4
MEM_HEAD_DIM = D_MODEL // MEM_HEADS
N_EXPERTS = 16
EC_CAPACITY_FACTOR = 2
D_FF = 2 * D_MODEL
LN_EPS = 1e-5
RMS_EPS = 1e-5
DEPTH = 1
DEEPNORM_ALPHA = (2 * DEPTH) ** 0.25

LANES = 128
SUBLANES = 8
NEG = -1e30
VMEM_LIMIT = 56 * 1024 * 1024

NA_QROWS = 8
NA_SLAB = 16
NA_SUB = 4
GLA_SUPER = 512
GLA_SEQS = 1
TOK_TILE = 512
SUB_TILE = 256
FFN_ROWS = 512
FFN_SLOTS = 3
CMB_TILE = 256
CMB_WIN = 64


def _dot(a, b):
    return jnp.dot(a, b, preferred_element_type=F32)


def _dot_nt(a, b):
    return lax.dot_general(a, b, (((1,), (1,)), ((), ())), preferred_element_type=F32)


def _dot_tn(a, b):
    return lax.dot_general(a, b, (((0,), (0,)), ((), ())), preferred_element_type=F32)


def _layer_norm(x, g, b):
    mu = jnp.mean(x, axis=-1, keepdims=True)
    xc = x - mu
    var = jnp.mean(xc * xc, axis=-1, keepdims=True)
    return xc * lax.rsqrt(var + LN_EPS) * g + b


def _split3(x):
    hi = x.astype(BF16)
    r1 = x - hi.astype(F32)
    mid = r1.astype(BF16)
    lo = (r1 - mid.astype(F32)).astype(BF16)
    return hi, mid, lo


def _params(*sem):
    return pltpu.CompilerParams(dimension_semantics=sem, vmem_limit_bytes=VMEM_LIMIT)


def _const_spec(shape):
    nd = len(shape)
    return pl.BlockSpec(shape, lambda *_: (0,) * nd)


def _in_proj_kernel(x_ref, g_ref, b_ref, wna_ref, wqk_ref, wv_ref, wr_ref, wgk_ref, w2_ref, gb_ref,
                    h0_ref, qkv_ref, qk_ref, v_ref, r_ref, la_ref):
    for r0 in range(0, TOK_TILE, SUB_TILE):
        rows = slice(r0, r0 + SUB_TILE)
        h0 = _layer_norm(x_ref[rows, :], g_ref[...], b_ref[...])
        h0_ref[rows, :] = h0
        hb = h0.astype(BF16)
        qkv_ref[rows, :] = _dot(hb, wna_ref[...]).astype(BF16)
        qk_ref[rows, :] = _dot(hb, wqk_ref[...])
        v_ref[rows, :] = _dot(hb, wv_ref[...]).astype(BF16)
        r_ref[rows, :] = _dot(hb, wr_ref[...])
        gk = _dot(hb, wgk_ref[...])
        z = _dot(gk.astype(BF16), w2_ref[...]) + gb_ref[...]
        log_sig = jnp.minimum(z, 0.0) - jnp.log1p(jnp.exp(-jnp.abs(z)))
        la_ref[rows, :] = log_sig * (1.0 / GLA_GATE_NORM)


def _in_proj(x, g, b, wna, wqk, wv, wr, wgk, w2, gb):
    n = x.shape[0]
    tm = TOK_TILE
    row = lambda w: pl.BlockSpec((tm, w), lambda i: (i, 0))
    outs = (
        jax.ShapeDtypeStruct((n, D_MODEL), F32),
        jax.ShapeDtypeStruct((n, 3 * NA_WIDTH), BF16),
        jax.ShapeDtypeStruct((n, 2 * GLA_KEY_WIDTH), F32),
        jax.ShapeDtypeStruct((n, GLA_WIDTH), BF16),
        jax.ShapeDtypeStruct((n, GLA_WIDTH), F32),
        jax.ShapeDtypeStruct((n, 2 * GLA_KEY_WIDTH), F32),
    )
    return pl.pallas_call(
        _in_proj_kernel,
        grid=(n // tm,),
        in_specs=[row(D_MODEL), _const_spec(g.shape), _const_spec(b.shape), _const_spec(wna.shape),
                  _const_spec(wqk.shape), _const_spec(wv.shape), _const_spec(wr.shape),
                  _const_spec(wgk.shape), _const_spec(w2.shape), _const_spec(gb.shape)],
        out_specs=(row(D_MODEL), row(3 * NA_WIDTH), row(2 * GLA_KEY_WIDTH), row(GLA_WIDTH), row(GLA_WIDTH),
                   row(2 * GLA_KEY_WIDTH)),
        out_shape=outs,
        name="ln_in_proj",
        compiler_params=_params("parallel"),
    )(x, g, b, wna, wqk, wv, wr, wgk, w2, gb)


def _na_tables(rpb):
    kw = NA_WIN_COLS
    cols = np.arange(GRID_W)
    col_start = np.clip(cols - kw // 2, 0, GRID_W - kw)
    j = np.arange(GRID_W)
    col_ok = (j[None, :] >= col_start[:, None]) & (j[None, :] < col_start[:, None] + kw)
    col_off = np.clip(j[None, :] - cols[:, None] + (NA_WIN_COLS - 1), 0, 2 * NA_WIN_COLS - 2)
    nblk = NA_SLAB + NA_QROWS
    ro = np.arange(nblk) - 5
    ro_ok = (ro >= 0) & (ro < 2 * NA_WIN_ROWS - 1)
    ro_c = np.clip(ro, 0, 2 * NA_WIN_ROWS - 2)
    t = rpb[:, ro_c][:, :, col_off]
    ok = jnp.asarray(ro_ok[None, :, None, None] & col_ok[None, None])
    t = jnp.where(ok, t, NEG).astype(F32)
    flat = jnp.transpose(t, (0, 2, 1, 3)).reshape(NA_HEADS, GRID_W, nblk * GRID_W)
    shifted = jnp.concatenate([flat[:, :, GRID_W:], jnp.full((NA_HEADS, GRID_W, GRID_W), NEG, F32)], axis=-1)
    tab = jnp.stack([flat, shifted])

    a = np.arange(NA_QROWS)
    i = np.arange(NA_SLAB)
    kaug = np.zeros((3, 2, NA_SLAB * GRID_W, LANES), np.float32)
    for vi, w in enumerate((np.maximum(a, 4), a, np.minimum(a, 4))):
        valid = (i[None, :] >= w[:, None]) & (i[None, :] < w[:, None] + NA_WIN_ROWS)
        per_key = np.repeat(np.where(valid, 0.0, NEG).T, GRID_W, axis=0)
        kaug[vi, 0, :, NA_HEAD_DIM:NA_HEAD_DIM + NA_QROWS] = per_key
        kaug[vi, 1, :, :NA_QROWS] = per_key
    return tab, jnp.asarray(kaug).astype(BF16)


def _na_kernel(q_ref, kp_ref, kc_ref, kn_ref, vp_ref, vc_ref, vn_ref, tab_ref, kaug_ref, o_ref):
    half = NA_QROWS * GRID_W // 2
    k = jnp.concatenate([kp_ref[half:, :], kc_ref[...], kn_ref[:half, :]], axis=0)
    v = jnp.concatenate([vp_ref[half:, :], vc_ref[...], vn_ref[:half, :]], axis=0)
    nq = NA_SUB * GRID_W
    nk = (NA_SUB + NA_WIN_ROWS) * GRID_W
    qlane = lax.broadcasted_iota(I32, (nq, LANES), 1)
    qrow = lax.broadcasted_iota(I32, (nq, LANES), 0) // GRID_W
    klane = lax.broadcasted_iota(I32, (nk, LANES), 1)
    first = qlane < NA_HEAD_DIM
    for hp in range(NA_HEADS // 2):
        sl = slice(hp * LANES, (hp + 1) * LANES)
        for u in range(NA_QROWS // NA_SUB):
            r0 = u * nq
            q2 = q_ref[r0:r0 + nq, sl] * (NA_HEAD_DIM ** -0.5)
            k2 = k[r0:r0 + nk, sl]
            v2 = v[r0:r0 + nk, sl]
            pv = []
            for sub in range(2):
                h = 2 * hp + sub
                aug0 = NA_HEAD_DIM if sub == 0 else 0
                onehot = (qlane == aug0 + u * NA_SUB + qrow).astype(BF16)
                qm = jnp.where(first if sub == 0 else jnp.logical_not(first), q2, onehot)
                in_aug = jnp.logical_and(klane >= aug0, klane < aug0 + NA_QROWS)
                km = jnp.where(in_aug, kaug_ref[0, sub, r0:r0 + nk, :], k2)
                s = _dot_nt(qm, km)
                strips = []
                for al in range(NA_SUB):
                    a = u * NA_SUB + al
                    par = a % 2
                    off = ((8 - a) if par == 0 else (7 - a)) * GRID_W + r0
                    strips.append(s[al * GRID_W:(al + 1) * GRID_W, :] + tab_ref[par, h, :, off:off + nk])
                s = jnp.concatenate(strips, axis=0)
                m = jnp.max(s, axis=-1, keepdims=True)
                e = jnp.exp(s - m)
                l = jnp.sum(e, axis=-1, keepdims=True)
                pv.append(_dot(e.astype(BF16), v2) / l)
            o_ref[r0:r0 + nq, sl] = jnp.where(first, pv[0], pv[1]).astype(BF16)


def _na(qkv, tab, kaug, batch, seq):
    n = batch * seq
    nq = NA_QROWS * GRID_W
    nblk = seq // nq
    assert seq % nq == 0 and nblk >= 2
    cur = lambda c: pl.BlockSpec((nq, NA_WIDTH), lambda b, r: (b * nblk + r, c))
    prv = lambda c: pl.BlockSpec((nq, NA_WIDTH), lambda b, r: (b * nblk + jnp.maximum(r - 1, 0), c))
    nxt = lambda c: pl.BlockSpec((nq, NA_WIDTH), lambda b, r: (b * nblk + jnp.minimum(r + 1, nblk - 1), c))
    variant = lambda b, r: (jnp.where(r == 0, 0, jnp.where(r == nblk - 1, 2, 1)), 0, 0, 0)
    return pl.pallas_call(
        _na_kernel,
        grid=(batch, nblk),
        in_specs=[cur(0), prv(1), cur(1), nxt(1), prv(2), cur(2), nxt(2),
                  _const_spec(tab.shape), pl.BlockSpec((1,) + kaug.shape[1:], variant)],
        out_specs=pl.BlockSpec((nq, NA_WIDTH), lambda b, r: (b * nblk + r, 0)),
        out_shape=jax.ShapeDtypeStruct((n, NA_WIDTH), BF16),
        name="nbr_attn",
        compiler_params=_params("parallel", "parallel"),
    )(qkv, qkv, qkv, qkv, qkv, qkv, qkv, tab, kaug)


def _gla_kernel(qkf_ref, vf_ref, laf_ref, qkb_ref, vb_ref, lab_ref, of_ref, ob_ref, sf_ref, sb_ref):
    c = GLA_CHUNK
    kwid = GLA_KEY_WIDTH
    nchunk = GLA_SUPER // c

    @pl.when(pl.program_id(1) == 0)
    def _():
        sf_ref[...] = jnp.zeros_like(sf_ref)
        sb_ref[...] = jnp.zeros_like(sb_ref)

    ri = lax.broadcasted_iota(I32, (GLA_SUPER, GLA_SUPER), 0)
    ci = lax.broadcasted_iota(I32, (GLA_SUPER, GLA_SUPER), 1)
    same_chunk = (ri // c) == (ci // c)
    tri_f = jnp.logical_and(same_chunk, ci <= ri).astype(BF16)
    tri_b = jnp.logical_and(same_chunk, ci >= ri).astype(BF16)
    ar = lax.broadcasted_iota(I32, (c, kwid), 0)
    ac = lax.broadcasted_iota(I32, (c, kwid), 1) % c
    keep_f = ac <= ar
    keep_b = ac >= ar
    wr_ = lax.broadcasted_iota(I32, (kwid, kwid), 0) // c
    wc_ = lax.broadcasted_iota(I32, (kwid, kwid), 1) // GLA_DK
    wmask = wr_ == wc_
    vr_ = lax.broadcasted_iota(I32, (kwid, GLA_WIDTH), 0) // c
    vc_ = lax.broadcasted_iota(I32, (kwid, GLA_WIDTH), 1) // GLA_DV
    vmask = vr_ == vc_
    sr_ = lax.broadcasted_iota(I32, (GLA_WIDTH, kwid), 0) // GLA_DV
    sc_ = lax.broadcasted_iota(I32, (GLA_WIDTH, kwid), 1) // GLA_DK
    smask = sr_ == sc_

    def stream(qk_ref, v_ref, la_ref, o_ref, st_ref, tri, keep, last, order):
        hi, mid, lo = _split3(la_ref[...])
        b_all = _dot(tri, hi) + _dot(tri, mid) + _dot(tri, lo)
        qes, intra, upds, decs = [], [], [], []
        for ic in range(nchunk):
            r = slice(ic * c, (ic + 1) * c)
            b = b_all[r]
            bl = b[last:last + 1, :]
            q = qk_ref[r, :kwid] * (GLA_DK ** -0.5)
            k = qk_ref[r, kwid:]
            v = v_ref[r, :]
            qe = (q * jnp.exp(b)).astype(BF16)
            ke = (k * jnp.exp(-b)).astype(BF16)
            kd = (k * jnp.exp(bl - b)).astype(BF16)
            wt = jnp.where(wmask, jnp.concatenate([ke] * GLA_HEADS, axis=0), jnp.zeros((kwid, kwid), BF16))
            a = jnp.where(keep, _dot_nt(qe, wt), 0.0)
            vbd = jnp.where(vmask, jnp.concatenate([v] * GLA_HEADS, axis=0),
                            jnp.zeros((kwid, GLA_WIDTH), BF16))
            qes.append(qe)
            intra.append(_dot(a.astype(BF16), vbd))
            upds.append(jnp.where(smask, _dot_tn(v, kd), 0.0))
            decs.append(jnp.exp(bl))
        st = st_ref[...]
        for ic in order:
            o_ref[ic * c:(ic + 1) * c, :] = intra[ic] + _dot_nt(qes[ic], st.astype(BF16))
            st = st * decs[ic] + upds[ic]
        st_ref[...] = st

    for j in range(GLA_SEQS):
        stream(qkf_ref.at[j], vf_ref.at[j], laf_ref.at[j], of_ref.at[j], sf_ref.at[j],
               tri_f, keep_f, c - 1, range(nchunk))
        stream(qkb_ref.at[j], vb_ref.at[j], lab_ref.at[j], ob_ref.at[j], sb_ref.at[j],
               tri_b, keep_b, 0, range(nchunk - 1, -1, -1))


def _gla(qk, v, la, batch, seq):
    n = batch * seq
    ns = seq // GLA_SUPER
    g = GLA_SEQS
    assert seq % GLA_SUPER == 0 and batch % g == 0
    fwd = lambda w, c: pl.BlockSpec((g, GLA_SUPER, w), lambda b, s: (b, s, c))
    bwd = lambda w, c: pl.BlockSpec((g, GLA_SUPER, w), lambda b, s: (b, ns - 1 - s, c))
    kw2 = 2 * GLA_KEY_WIDTH
    seqs = lambda t: t.reshape(batch, seq, t.shape[-1])
    state = pltpu.VMEM((g, GLA_WIDTH, GLA_KEY_WIDTH), F32)
    of, ob = pl.pallas_call(
        _gla_kernel,
        grid=(batch // g, ns),
        in_specs=[fwd(kw2, 0), fwd(GLA_WIDTH, 0), fwd(GLA_KEY_WIDTH, 0),
                  bwd(kw2, 0), bwd(GLA_WIDTH, 0), bwd(GLA_KEY_WIDTH, 1)],
        out_specs=(fwd(GLA_WIDTH, 0), bwd(GLA_WIDTH, 0)),
        out_shape=(jax.ShapeDtypeStruct((batch, seq, GLA_WIDTH), F32),
                   jax.ShapeDtypeStruct((batch, seq, GLA_WIDTH), F32)),
        scratch_shapes=[state, state],
        name="gla",
        compiler_params=_params("parallel", "arbitrary"),
    )(seqs(qk), seqs(v), seqs(la), seqs(qk), seqs(v), seqs(la))
    return of.reshape(n, GLA_WIDTH), ob.reshape(n, GLA_WIDTH)


def _mix_out_kernel(na_ref, of_ref, ob_ref, r_ref, ng_ref, h0_ref, w1_ref, w2_ref, g_ref, b_ref, h1_ref):
    o = of_ref[...] + ob_ref[...]
    parts = []
    for h in range(GLA_HEADS):
        oh = o[:, h * GLA_DV:(h + 1) * GLA_DV]
        ms = jnp.mean(oh * oh, axis=-1, keepdims=True)
        parts.append(oh * lax.rsqrt(ms + RMS_EPS) * ng_ref[...])
    r = r_ref[...]
    gl = jnp.concatenate(parts, axis=-1) * (r * jax.nn.sigmoid(r))
    mixed = _dot(na_ref[...], w1_ref[...]) + _dot(gl.astype(BF16), w2_ref[...])
    h1_ref[...] = _layer_norm(DEEPNORM_ALPHA * h0_ref[...] + mixed, g_ref[...], b_ref[...])


def _mix_out(na, of, ob, r, ng, h0, w1, w2, g, b):
    n = na.shape[0]
    tm = TOK_TILE
    row = lambda w: pl.BlockSpec((tm, w), lambda i: (i, 0))
    return pl.pallas_call(
        _mix_out_kernel,
        grid=(n // tm,),
        in_specs=[row(NA_WIDTH), row(GLA_WIDTH), row(GLA_WIDTH), row(GLA_WIDTH), _const_spec(ng.shape),
                  row(D_MODEL), _const_spec(w1.shape), _const_spec(w2.shape), _const_spec(g.shape),
                  _const_spec(b.shape)],
        out_specs=row(D_MODEL),
        out_shape=jax.ShapeDtypeStruct((n, D_MODEL), F32),
        name="mix_out_ln1",
        compiler_params=_params("parallel"),
    )(na, of, ob, r, ng, h0, w1, w2, g, b)


def _mem_kv_kernel(m_ref, w_ref, o_ref):
    o_ref[...] = _dot(m_ref[...].astype(BF16), w_ref[...]).astype(BF16)


def _mem_kv(mem, wkv):
    rows = mem.shape[0]
    tn = D_MODEL
    return pl.pallas_call(
        _mem_kv_kernel,
        grid=(rows // MEM_TOKENS, wkv.shape[1] // tn),
        in_specs=[pl.BlockSpec((MEM_TOKENS, D_MODEL), lambda i, j: (i, 0)),
                  pl.BlockSpec((D_MODEL, tn), lambda i, j: (0, j))],
        out_specs=pl.BlockSpec((MEM_TOKENS, tn), lambda i, j: (i, j)),
        out_shape=jax.ShapeDtypeStruct((rows, wkv.shape[1]), BF16),
        name="mem_kv",
        compiler_params=_params("parallel", "parallel"),
    )(mem, wkv)


def _xattn_kernel(h1_ref, kv_ref, wq_ref, wo_ref, g_ref, b_ref, wrh_ref, wrl_ref, h2_ref, h2t_ref, aff_ref):
    h1 = h1_ref[...]
    q = _dot(h1.astype(BF16), wq_ref[...]).astype(BF16)
    outs = []
    for h in range(MEM_HEADS):
        sl = slice(h * MEM_HEAD_DIM, (h + 1) * MEM_HEAD_DIM)
        kh = kv_ref[:, sl]
        vh = kv_ref[:, D_MODEL + h * MEM_HEAD_DIM:D_MODEL + (h + 1) * MEM_HEAD_DIM]
        s = _dot_nt(q[:, sl], kh) * (MEM_HEAD_DIM ** -0.5)
        m = jnp.max(s, axis=-1, keepdims=True)
        e = jnp.exp(s - m)
        p = (e / jnp.sum(e, axis=-1, keepdims=True)).astype(BF16)
        outs.append(_dot(p, vh))
    o = jnp.concatenate(outs, axis=-1).astype(BF16)
    h2 = _layer_norm(DEEPNORM_ALPHA * h1 + _dot(o, wo_ref[...]), g_ref[...], b_ref[...])
    h2_ref[...] = h2
    for c in range(D_MODEL // LANES):
        h2t_ref[:, c, :] = h2[:, c * LANES:(c + 1) * LANES]
    hh = h2.astype(BF16)
    hl = (h2 - hh.astype(F32)).astype(BF16)
    logits = _dot_nt(wrh_ref[...], hh) + _dot_nt(wrh_ref[...], hl) + _dot_nt(wrl_ref[...], hh)
    m = jnp.max(logits, axis=0, keepdims=True)
    e = jnp.exp(logits - m)
    aff_ref[...] = e / jnp.sum(e, axis=0, keepdims=True)


def _xattn(h1, kv, wq, wo, g, b, wrh, wrl, batch, seq):
    n = batch * seq
    tm = TOK_TILE
    nt = seq // tm
    row = pl.BlockSpec((tm, D_MODEL), lambda i: (i, 0))
    return pl.pallas_call(
        _xattn_kernel,
        grid=(n // tm,),
        in_specs=[row, pl.BlockSpec((MEM_TOKENS, 2 * D_MODEL), lambda i: (i // nt, 0)),
                  _const_spec(wq.shape), _const_spec(wo.shape), _const_spec(g.shape), _const_spec(b.shape),
                  _const_spec(wrh.shape), _const_spec(wrl.shape)],
        out_specs=(row, pl.BlockSpec((tm, D_MODEL // LANES, LANES), lambda i: (i, 0, 0)),
                   pl.BlockSpec((N_EXPERTS, tm), lambda i: (0, i))),
        out_shape=(jax.ShapeDtypeStruct((n, D_MODEL), F32),
                   jax.ShapeDtypeStruct((n, D_MODEL // LANES, LANES), F32),
                   jax.ShapeDtypeStruct((N_EXPERTS, n), F32)),
        name="xattn_ln2_router",
        compiler_params=_params("parallel"),
    )(h1, kv, wq, wo, g, b, wrh, wrl)


def _route_kernel(aff_ref, idx_ref, gate_ref, pos_ref, bst_ref, *, cap, pchunk):
    a = aff_ref[0]
    nb = a.shape[0]

    def total(x):
        return jnp.sum(jnp.sum(x, axis=0, keepdims=True), axis=1, keepdims=True)

    def largest(n_iter, hi0, admits):
        def step(_, carry):
            lo, hi = carry
            mid = lo + ((hi - lo + 1) >> 1)
            ok = admits(mid)
            return jnp.where(ok, mid, lo), jnp.where(ok, hi, mid - 1)
        return lax.fori_loop(0, n_iter, step, (jnp.zeros((1, 1), I32), jnp.full((1, 1), hi0, I32)))[0]

    tbits = largest(31, 0x7F800000,
                    lambda m: total((a >= lax.bitcast_convert_type(m, F32)).astype(F32)) >= cap)
    thr = lax.bitcast_convert_type(tbits, F32)
    d = a - thr
    fine = lax.bitcast_convert_type(jnp.maximum((tbits >> 23) - 47, 1) << 23, F32)
    kfine = largest(24, (1 << 24) - 1, lambda m: total((d >= m.astype(F32) * fine).astype(F32)) >= cap)
    dthr = kfine.astype(F32) * fine

    li = lax.broadcasted_iota(I32, (LANES, LANES), 0)
    lj = lax.broadcasted_iota(I32, (LANES, LANES), 1)
    upper = (li <= lj).astype(BF16)
    bi = lax.broadcasted_iota(I32, (nb, nb), 0)
    bj = lax.broadcasted_iota(I32, (nb, nb), 1)
    strict = (bj < bi).astype(BF16)
    incl = (bi <= bj).astype(BF16)
    ones8 = jnp.ones((8, LANES), BF16)

    def prefix(mask_b):
        rowcum = _dot(mask_b, upper)
        rowtot = jnp.broadcast_to(rowcum[:, LANES - 1:LANES], (nb, LANES)).astype(BF16)
        return rowcum + _dot(strict, rowtot)

    gt = d > dthr
    eq = d == dthr
    need = cap - total(gt.astype(F32))
    sel = jnp.logical_or(gt, jnp.logical_and(eq, prefix(eq.astype(BF16)) <= need))
    selb = sel.astype(BF16)
    cnt = prefix(selb)

    rt_row = _dot_nt(ones8, selb)
    bend = _dot(rt_row.astype(BF16), incl)
    bstart = bend - rt_row
    pos_ref[0] = jnp.where(sel, cnt - 1.0, -1.0).astype(I32)
    bst_ref[0] = bstart[0:1, :].astype(I32)

    cnt_hi = jnp.floor(cnt * (1.0 / 64.0))
    cnt_lo = cnt - 64.0 * cnt_hi
    a_hi, a_mid, a_lo = _split3(a)
    rowid = lax.broadcasted_iota(I32, (nb, LANES), 0).astype(BF16)
    rhs = jnp.concatenate([cnt_hi.astype(BF16), cnt_lo.astype(BF16), selb, rowid, a_hi, a_mid, a_lo], axis=1)
    lanef = lax.broadcasted_iota(I32, (pchunk, LANES), 1).astype(F32)

    for pc in range(cap // pchunk):
        p = (lax.broadcasted_iota(I32, (pchunk, nb), 0) + pc * pchunk).astype(F32)
        g = jnp.logical_and(bstart[0:1, :] <= p, p < bend[0:1, :]).astype(BF16)
        rows = _dot(g, rhs)
        part = lambda k: rows[:, k * LANES:(k + 1) * LANES]
        crow = part(0) * 64.0 + part(1)
        p1 = (lax.broadcasted_iota(I32, (pchunk, LANES), 0) + (pc * pchunk + 1)).astype(F32)
        oh = jnp.logical_and(crow == p1, part(2) > 0.5)
        pick = lambda x: jnp.where(oh, x, 0.0)
        tok_row = _dot_nt(ones8, pick(part(3)).astype(BF16))
        tok_lane = _dot_nt(ones8, pick(lanef).astype(BF16))
        idx_ref[0, :, pc * pchunk:(pc + 1) * pchunk] = (tok_row[0:1] * float(LANES) + tok_lane[0:1]).astype(I32)
        gate = jnp.sum(pick(part(4) + part(5) + part(6)), axis=1, keepdims=True)
        gate_ref[pc * pchunk:(pc + 1) * pchunk, :] = jnp.broadcast_to(gate, (pchunk, LANES))


def _route(aff_t, cap):
    e, n = aff_t.shape
    nb = n // LANES
    assert nb <= 256 and cap % 64 == 0 and cap // 64 <= 256
    pchunk = min(1024, cap)
    aff3 = aff_t.reshape(e, nb, LANES)
    return pl.pallas_call(
        functools.partial(_route_kernel, cap=cap, pchunk=pchunk),
        grid=(e,),
        in_specs=[pl.BlockSpec((1, nb, LANES), lambda i: (i, 0, 0))],
        out_specs=(pl.BlockSpec((1, 1, cap), lambda i: (i, 0, 0)),
                   pl.BlockSpec((cap, LANES), lambda i: (i, 0)),
                   pl.BlockSpec((1, nb, LANES), lambda i: (i, 0, 0)),
                   pl.BlockSpec((1, 1, nb), lambda i: (i, 0, 0))),
        out_shape=(jax.ShapeDtypeStruct((e, 1, cap), I32), jax.ShapeDtypeStruct((e * cap, LANES), F32),
                   jax.ShapeDtypeStruct((e, nb, LANES), I32), jax.ShapeDtypeStruct((e, 1, nb), I32)),
        name="route",
        compiler_params=_params("parallel"),
    )(aff3)


def _ffn_kernel(idx_ref, nxt_ref, nx2_ref, gate_ref, wg_ref, wu_ref, wd_ref, h2_hbm, ye_ref, xbuf, sem):
    s = pl.program_id(0)
    slot = s % FFN_SLOTS
    ahead = (s + FFN_SLOTS - 1) % FFN_SLOTS

    def row_copy(rows_ref, dst_slot, i):
        return pltpu.make_async_copy(h2_hbm.at[rows_ref[0, 0, i]], xbuf.at[dst_slot, i], sem.at[dst_slot])

    def wait_slot(dst_slot):
        pltpu.make_async_copy(h2_hbm.at[pl.ds(0, FFN_ROWS)], xbuf.at[dst_slot], sem.at[dst_slot]).wait()

    @pl.when(s == 0)
    def _():
        def body(i, carry):
            row_copy(idx_ref, 0, i).start()
            row_copy(nxt_ref, 1, i).start()
            return carry
        lax.fori_loop(0, FFN_ROWS, body, 0, unroll=8)

    wait_slot(slot)
    x = jnp.concatenate([xbuf[slot, :, c, :] for c in range(D_MODEL // LANES)], axis=-1).astype(BF16)
    for i in range(FFN_ROWS):
        row_copy(nx2_ref, ahead, i).start(priority=i % 2)
    a = _dot(x, wg_ref[0])
    u = _dot(x, wu_ref[0])
    hmid = (a * jax.nn.sigmoid(a) * u).astype(BF16)
    ye_ref[...] = _dot(hmid, wd_ref[0]) * gate_ref[:, 0:1]

    @pl.when(s + 1 == pl.num_programs(0))
    def _():
        wait_slot((s + 1) % FFN_SLOTS)
        wait_slot(ahead)


def _ffn(idx, gate, wg, wu, wd, h2, cap):
    e = idx.shape[0]
    tr = FFN_ROWS
    per = cap // tr
    nsteps = e * per
    assert cap % tr == 0 and FFN_SLOTS == 3 and nsteps >= FFN_SLOTS
    wspec = lambda shp: pl.BlockSpec((1,) + shp, lambda s: (s // per, 0, 0))
    rows = lambda shift: pl.BlockSpec(
        (1, 1, tr), lambda s: (jnp.minimum(s + shift, nsteps - 1) // per, 0, jnp.minimum(s + shift, nsteps - 1) % per),
        memory_space=pltpu.SMEM)
    return pl.pallas_call(
        _ffn_kernel,
        grid=(nsteps,),
        in_specs=[rows(0), rows(1), rows(2), pl.BlockSpec((tr, LANES), lambda s: (s, 0)),
                  wspec((D_MODEL, D_FF)), wspec((D_MODEL, D_FF)), wspec((D_FF, D_MODEL)),
                  pl.BlockSpec(memory_space=pl.ANY)],
        out_specs=pl.BlockSpec((tr, D_MODEL), lambda s: (s, 0)),
        out_shape=jax.ShapeDtypeStruct((e * cap, D_MODEL), F32),
        scratch_shapes=[pltpu.VMEM((FFN_SLOTS, tr, D_MODEL // LANES, LANES), F32),
                        pltpu.SemaphoreType.DMA((FFN_SLOTS,))],
        name="ffn",
        compiler_params=_params("arbitrary"),
    )(idx, idx, idx, gate, wg, wu, wd, h2)


def _combine_kernel(bst_ref, pos_ref, h2_ref, g_ref, b_ref, ye_hbm, o_ref, ybuf, sem, *, cap):
    e_n = N_EXPERTS
    w = CMB_WIN
    step = w - SUBLANES
    per = CMB_TILE // LANES
    t = pl.program_id(0)
    last = pl.num_programs(0) - 1
    slot = t % 2
    wi = lax.broadcasted_iota(I32, (w, CMB_TILE), 0)

    def windows(tile, r):
        lo = [bst_ref[e, tile * per] + r * step for e in range(e_n)]
        src = [pl.multiple_of(jnp.minimum(e * cap + (lo[e] // SUBLANES) * SUBLANES, e_n * cap - w), SUBLANES)
               for e in range(e_n)]
        return lo, src

    def copies(src, dst_slot):
        return [pltpu.make_async_copy(ye_hbm.at[pl.ds(src[e], w)], ybuf.at[dst_slot, pl.ds(e * w, w)],
                                      sem.at[dst_slot, e]) for e in range(e_n)]

    def contribution(lo, src, buf_slot):
        onehot = []
        for e in range(e_n):
            base = src[e] - e * cap
            p = pos_ref[e:e + 1, :]
            this_round = jnp.logical_and(p >= lo[e], p < lo[e] + step)
            onehot.append(jnp.logical_and(p == base + wi, this_round).astype(BF16))
        oh_t = jnp.concatenate(onehot, axis=0)
        yv = ybuf[buf_slot]
        hi = yv.astype(BF16)
        lo_part = (yv - hi.astype(F32)).astype(BF16)
        return _dot_tn(oh_t, hi) + _dot_tn(oh_t, lo_part)

    lo0, src0 = windows(t, 0)

    @pl.when(t == 0)
    def _():
        for c in copies(src0, 0):
            c.start()

    _, src_next = windows(jnp.minimum(t + 1, last), 0)
    for c in copies(src_next, 1 - slot):
        c.start()
    for c in copies(src0, slot):
        c.wait()
    y = contribution(lo0, src0, slot)

    counts = [bst_ref[e, (t + 1) * per] - bst_ref[e, t * per] for e in range(e_n)]
    nrounds = functools.reduce(jnp.maximum, [(c + (step - 1)) // step for c in counts])

    def extra_round(r, y):
        lo, src = windows(t, r)
        for c in copies(src, slot):
            c.start()
        for c in copies(src, slot):
            c.wait()
        return y + contribution(lo, src, slot)

    y = lax.fori_loop(1, nrounds, extra_round, y)
    o_ref[...] = _layer_norm(DEEPNORM_ALPHA * h2_ref[...] + y, g_ref[...], b_ref[...])

    @pl.when(t == last)
    def _():
        for c in copies(src_next, 1 - slot):
            c.wait()


def _combine(bst, pos, h2, g, b, ye, cap):
    n = h2.shape[0]
    tt = CMB_TILE
    assert cap >= CMB_WIN and n % tt == 0
    grid_spec = pltpu.PrefetchScalarGridSpec(
        num_scalar_prefetch=1,
        grid=(n // tt,),
        in_specs=[pl.BlockSpec((N_EXPERTS, tt), lambda i, *_: (0, i)),
                  pl.BlockSpec((tt, D_MODEL), lambda i, *_: (i, 0)),
                  pl.BlockSpec(g.shape, lambda i, *_: (0, 0)), pl.BlockSpec(b.shape, lambda i, *_: (0, 0)),
                  pl.BlockSpec(memory_space=pl.ANY)],
        out_specs=pl.BlockSpec((tt, D_MODEL), lambda i, *_: (i, 0)),
        scratch_shapes=[pltpu.VMEM((2, N_EXPERTS * CMB_WIN, D_MODEL), F32),
                        pltpu.SemaphoreType.DMA((2, N_EXPERTS))],
    )
    return pl.pallas_call(
        functools.partial(_combine_kernel, cap=cap),
        grid_spec=grid_spec,
        out_shape=jax.ShapeDtypeStruct((n, D_MODEL), F32),
        name="combine_ln3",
        compiler_params=_params("arbitrary"),
    )(bst, pos, h2, g, b, ye)


def _prepare(ln_in_g, ln_in_b, w_in, na_rpb, gla_gate_w2, gla_gate_b, gla_norm_g, w_out, ln1_g, ln1_b,
             mem_wq, mem_wkv, mem_wo, ln2_g, ln2_b, w_router, w_gate, w_up, w_down, ln3_g, ln3_b):
    row = lambda v: v.reshape(1, -1).astype(F32)
    w = w_in[0]
    o = np.cumsum((0, NA_WIDTH, NA_WIDTH, NA_WIDTH, GLA_KEY_WIDTH, GLA_KEY_WIDTH, GLA_WIDTH, GLA_WIDTH,
                   2 * GLA_GATE_RANK))
    wgk = jnp.zeros((D_MODEL, LANES), F32).at[:, :2 * GLA_GATE_RANK].set(w[:, o[7]:o[8]])
    w2 = jnp.zeros((LANES, 2 * GLA_KEY_WIDTH), F32)
    for s in range(2):
        w2 = w2.at[s * GLA_GATE_RANK:(s + 1) * GLA_GATE_RANK,
                   s * GLA_KEY_WIDTH:(s + 1) * GLA_KEY_WIDTH].set(gla_gate_w2[0, s])
    tab, kaug = _na_tables(na_rpb[0])
    wr_t = w_router[0].T.astype(F32)
    wr_hi = wr_t.astype(BF16)
    wr_lo = (wr_t - wr_hi.astype(F32)).astype(BF16)
    return dict(
        ln_in=(row(ln_in_g), row(ln_in_b)),
        wna=w[:, o[0]:o[3]].astype(BF16), wqk=w[:, o[3]:o[5]].astype(BF16), wv=w[:, o[5]:o[6]].astype(BF16),
        wr=w[:, o[6]:o[7]].astype(BF16), wgk=wgk.astype(BF16), w2=w2.astype(BF16),
        gb=gla_gate_b[0].reshape(1, -1).astype(F32),
        tab=tab, kaug=kaug, ng=row(gla_norm_g[0]),
        wo1=w_out[0][:NA_WIDTH].astype(BF16), wo2=w_out[0][NA_WIDTH:].astype(BF16),
        ln1=(row(ln1_g[0]), row(ln1_b[0])),
        wq=mem_wq[0].astype(BF16), wkv=mem_wkv[0].astype(BF16), wmo=mem_wo[0].astype(BF16),
        ln2=(row(ln2_g[0]), row(ln2_b[0])),
        wr_hi=wr_hi, wr_lo=wr_lo,
        wg=w_gate[0].astype(BF16), wu=w_up[0].astype(BF16), wd=w_down[0].astype(BF16),
        ln3=(row(ln3_g[0]), row(ln3_b[0])),
    )


def _trunk(x, mem, p):
    batch, seq, _ = x.shape
    n = batch * seq
    cap = EC_CAPACITY_FACTOR * n // N_EXPERTS
    h0, qkv, qk, v, r, la = _in_proj(x.reshape(n, D_MODEL), *p["ln_in"], p["wna"], p["wqk"], p["wv"], p["wr"],
                                     p["wgk"], p["w2"], p["gb"])
    na = _na(qkv, p["tab"], p["kaug"], batch, seq)
    of, ob = _gla(qk, v, la, batch, seq)
    h1 = _mix_out(na, of, ob, r, p["ng"], h0, p["wo1"], p["wo2"], *p["ln1"])
    kv = _mem_kv(mem.reshape(batch * MEM_TOKENS, D_MODEL), p["wkv"])
    h2, h2t, aff_t = _xattn(h1, kv, p["wq"], p["wmo"], *p["ln2"], p["wr_hi"], p["wr_lo"], batch, seq)
    idx, gate, pos, bst = _route(aff_t, cap)
    ye = _ffn(idx, gate, p["wg"], p["wu"], p["wd"], h2t, cap)
    bst = jnp.concatenate([bst[:, 0, :], jnp.full((N_EXPERTS, 1), cap, I32)], axis=1)
    out = _combine(bst, pos.reshape(N_EXPERTS, n), h2, *p["ln3"], ye, cap)
    return out.reshape(batch, seq, D_MODEL)


def kernel(x_prompt, x_sample, mem_prompt, mem_sample, ln_in_g, ln_in_b, w_in, na_rpb, gla_gate_w2, gla_gate_b,
           gla_norm_g, w_out, ln1_g, ln1_b, mem_wq, mem_wkv, mem_wo, ln2_g, ln2_b, w_router, w_gate, w_up, w_down,
           ln3_g, ln3_b):
    p = _prepare(ln_in_g, ln_in_b, w_in, na_rpb, gla_gate_w2, gla_gate_b, gla_norm_g, w_out, ln1_g, ln1_b,
                 mem_wq, mem_wkv, mem_wo, ln2_g, ln2_b, w_router, w_gate, w_up, w_down, ln3_g, ln3_b)
    return _trunk(x_prompt, mem_prompt, p), _trunk(x_sample, mem_sample, p)
```

```python
import functools

import numpy as np
import jax
import jax.numpy as jnp
from jax import lax
from jax.experimental import pallas as pl
from jax.experimental.pallas import tpu as pltpu

F32 = jnp.float32
BF16 = jnp.bfloat16
I32 = jnp.int32

D_MODEL = 1024
GRID_W = 64
NA_HEADS = 8
NA_HEAD_DIM = 64
NA_WIDTH = NA_HEADS * NA_HEAD_DIM
NA_WIN_ROWS = 8
NA_WIN_COLS = 16
GLA_HEADS = 4
GLA_DK = 64
GLA_DV = 128
GLA_KEY_WIDTH = GLA_HEADS * GLA_DK
GLA_WIDTH = GLA_HEADS * GLA_DV
GLA_GATE_RANK = 16
GLA_GATE_NORM = 16.0
GLA_CHUNK = 64
MEM_TOKENS = 256
MEM_HEADS = 4
MEM_HEAD_DIM = D_MODEL // MEM_HEADS
N_EXPERTS = 16
EC_CAPACITY_FACTOR = 2
D_FF = 2 * D_MODEL
LN_EPS = 1e-5
RMS_EPS = 1e-5
DEPTH = 1
DEEPNORM_ALPHA = (2 * DEPTH) ** 0.25

LANES = 128
SUBLANES = 8
NEG = -1e30
VMEM_LIMIT = 56 * 1024 * 1024

NA_QROWS = 8
NA_SLAB = 16
NA_SUB = 4
NA_AHEAD = 1
GLA_SUPER = 512
GLA_SEQS = 1
TOK_TILE = 512
SUB_TILE = 256
XATTN_SUB = 512
FFN_ROWS = 512
FFN_SLOTS = 3
CMB_TILE = 256
CMB_WIN = 64


def _dot(a, b):
    return jnp.dot(a, b, preferred_element_type=F32)


def _dot_nt(a, b):
    return lax.dot_general(a, b, (((1,), (1,)), ((), ())), preferred_element_type=F32)


def _dot_tn(a, b):
    return lax.dot_general(a, b, (((0,), (0,)), ((), ())), preferred_element_type=F32)


def _layer_norm(x, g, b):
    mu = jnp.mean(x, axis=-1, keepdims=True)
    xc = x - mu
    var = jnp.mean(xc * xc, axis=-1, keepdims=True)
    return xc * lax.rsqrt(var + LN_EPS) * g + b


def _split3(x):
    hi = x.astype(BF16)
    r1 = x - hi.astype(F32)
    mid = r1.astype(BF16)
    lo = (r1 - mid.astype(F32)).astype(BF16)
    return hi, mid, lo


def _params(*sem):
    return pltpu.CompilerParams(dimension_semantics=sem, vmem_limit_bytes=VMEM_LIMIT)


def _const_spec(shape):
    nd = len(shape)
    return pl.BlockSpec(shape, lambda *_: (0,) * nd)


def _in_proj_kernel(x_ref, g_ref, b_ref, wna_ref, wqk_ref, wv_ref, wr_ref, wgk_ref, w2_ref, gb_ref,
                    h0_ref, qkv_ref, qk_ref, v_ref, r_ref, la_ref):
    for r0 in range(0, TOK_TILE, SUB_TILE):
        rows = slice(r0, r0 + SUB_TILE)
        h0 = _layer_norm(x_ref[rows, :], g_ref[...], b_ref[...])
        h0_ref[rows, :] = h0
        hb = h0.astype(BF16)
        qkv_ref[rows, :] = _dot(hb, wna_ref[...]).astype(BF16)
        qk_ref[rows, :] = _dot(hb, wqk_ref[...])
        v_ref[rows, :] = _dot(hb, wv_ref[...]).astype(BF16)
        r_ref[rows, :] = _dot(hb, wr_ref[...])
        gk = _dot(hb, wgk_ref[...])
        z = _dot(gk.astype(BF16), w2_ref[...]) + gb_ref[...]
        log_sig = jnp.minimum(z, 0.0) - jnp.log1p(jnp.exp(-jnp.abs(z)))
        la_ref[rows, :] = log_sig * (1.0 / GLA_GATE_NORM)


def _in_proj(x, g, b, wna, wqk, wv, wr, wgk, w2, gb):
    n = x.shape[0]
    tm = TOK_TILE
    row = lambda w: pl.BlockSpec((tm, w), lambda i: (i, 0))
    outs = (
        jax.ShapeDtypeStruct((n, D_MODEL), F32),
        jax.ShapeDtypeStruct((n, 3 * NA_WIDTH), BF16),
        jax.ShapeDtypeStruct((n, 2 * GLA_KEY_WIDTH), F32),
        jax.ShapeDtypeStruct((n, GLA_WIDTH), BF16),
        jax.ShapeDtypeStruct((n, GLA_WIDTH), F32),
        jax.ShapeDtypeStruct((n, 2 * GLA_KEY_WIDTH), F32),
    )
    return pl.pallas_call(
        _in_proj_kernel,
        grid=(n // tm,),
        in_specs=[row(D_MODEL), _const_spec(g.shape), _const_spec(b.shape), _const_spec(wna.shape),
                  _const_spec(wqk.shape), _const_spec(wv.shape), _const_spec(wr.shape),
                  _const_spec(wgk.shape), _const_spec(w2.shape), _const_spec(gb.shape)],
        out_specs=(row(D_MODEL), row(3 * NA_WIDTH), row(2 * GLA_KEY_WIDTH), row(GLA_WIDTH), row(GLA_WIDTH),
                   row(2 * GLA_KEY_WIDTH)),
        out_shape=outs,
        name="ln_in_proj",
        compiler_params=_params("parallel"),
    )(x, g, b, wna, wqk, wv, wr, wgk, w2, gb)


def _na_tables(rpb):
    kw = NA_WIN_COLS
    cols = np.arange(GRID_W)
    col_start = np.clip(cols - kw // 2, 0, GRID_W - kw)
    j = np.arange(GRID_W)
    col_ok = (j[None, :] >= col_start[:, None]) & (j[None, :] < col_start[:, None] + kw)
    col_off = np.clip(j[None, :] - cols[:, None] + (NA_WIN_COLS - 1), 0, 2 * NA_WIN_COLS - 2)
    nblk = NA_SLAB + NA_QROWS
    ro = np.arange(nblk) - 5
    ro_ok = (ro >= 0) & (ro < 2 * NA_WIN_ROWS - 1)
    ro_c = np.clip(ro, 0, 2 * NA_WIN_ROWS - 2)
    t = rpb[:, ro_c][:, :, col_off]
    ok = jnp.asarray(ro_ok[None, :, None, None] & col_ok[None, None])
    t = jnp.where(ok, t, NEG).astype(F32)
    flat = jnp.transpose(t, (0, 2, 1, 3)).reshape(NA_HEADS, GRID_W, nblk * GRID_W)
    shifted = jnp.concatenate([flat[:, :, GRID_W:], jnp.full((NA_HEADS, GRID_W, GRID_W), NEG, F32)], axis=-1)
    tab = jnp.stack([flat, shifted])

    a = np.arange(NA_QROWS)
    i = np.arange(NA_SLAB)
    kaug = np.zeros((3, 2, NA_SLAB * GRID_W, LANES), np.float32)
    for vi, w in enumerate((np.maximum(a, 4), a, np.minimum(a, 4))):
        valid = (i[None, :] >= w[:, None]) & (i[None, :] < w[:, None] + NA_WIN_ROWS)
        per_key = np.repeat(np.where(valid, 0.0, NEG).T, GRID_W, axis=0)
        kaug[vi, 0, :, NA_HEAD_DIM:NA_HEAD_DIM + NA_QROWS] = per_key
        kaug[vi, 1, :, :NA_QROWS] = per_key
    return tab, jnp.asarray(kaug).astype(BF16)


def _na_kernel(q_ref, kp_ref, kc_ref, kn_ref, vp_ref, vc_ref, vn_ref, tab_ref, kaug_ref, o_ref):
    half = NA_QROWS * GRID_W // 2
    k = jnp.concatenate([kp_ref[half:, :], kc_ref[...], kn_ref[:half, :]], axis=0)
    v = jnp.concatenate([vp_ref[half:, :], vc_ref[...], vn_ref[:half, :]], axis=0)
    nq = NA_SUB * GRID_W
    nk = (NA_SUB + NA_WIN_ROWS) * GRID_W
    qlane = lax.broadcasted_iota(I32, (nq, LANES), 1)
    qrow = lax.broadcasted_iota(I32, (nq, LANES), 0) // GRID_W
    klane = lax.broadcasted_iota(I32, (nk, LANES), 1)
    first = qlane < NA_HEAD_DIM

    def scores(hp, u, sub):
        sl = slice(hp * LANES, (hp + 1) * LANES)
        r0 = u * nq
        h = 2 * hp + sub
        q2 = q_ref[r0:r0 + nq, sl] * (NA_HEAD_DIM ** -0.5)
        aug0 = NA_HEAD_DIM if sub == 0 else 0
        onehot = (qlane == aug0 + u * NA_SUB + qrow).astype(BF16)
        qm = jnp.where(first if sub == 0 else jnp.logical_not(first), q2, onehot)
        in_aug = jnp.logical_and(klane >= aug0, klane < aug0 + NA_QROWS)
        km = jnp.where(in_aug, kaug_ref[0, sub, r0:r0 + nk, :], k[r0:r0 + nk, sl])
        s = _dot_nt(qm, km)
        strips = []
        for al in range(NA_SUB):
            a = u * NA_SUB + al
            par = a % 2
            off = ((8 - a) if par == 0 else (7 - a)) * GRID_W + r0
            strips.append(s[al * GRID_W:(al + 1) * GRID_W, :] + tab_ref[par, h, :, off:off + nk])
        return jnp.concatenate(strips, axis=0)

    def attend(s, hp, u):
        m = jnp.max(s, axis=-1, keepdims=True)
        e = jnp.exp(s - m)
        l = jnp.sum(e, axis=-1, keepdims=True)
        return _dot(e.astype(BF16), v[u * nq:u * nq + nk, hp * LANES:(hp + 1) * LANES]) / l

    units = [(hp, u, sub) for hp in range(NA_HEADS // 2) for u in range(NA_QROWS // NA_SUB) for sub in range(2)]
    pending = [scores(*units[j]) for j in range(NA_AHEAD)]
    pv = {}
    for i, (hp, u, sub) in enumerate(units):
        if i + NA_AHEAD < len(units):
            pending.append(scores(*units[i + NA_AHEAD]))
        pv[sub] = attend(pending.pop(0), hp, u)
        if sub == 1:
            o_ref[u * nq:(u + 1) * nq, hp * LANES:(hp + 1) * LANES] = jnp.where(first, pv[0], pv[1]).astype(BF16)


def _na(qkv, tab, kaug, batch, seq):
    n = batch * seq
    nq = NA_QROWS * GRID_W
    nblk = seq // nq
    assert seq % nq == 0 and nblk >= 2
    cur = lambda c: pl.BlockSpec((nq, NA_WIDTH), lambda b, r: (b * nblk + r, c))
    prv = lambda c: pl.BlockSpec((nq, NA_WIDTH), lambda b, r: (b * nblk + jnp.maximum(r - 1, 0), c))
    nxt = lambda c: pl.BlockSpec((nq, NA_WIDTH), lambda b, r: (b * nblk + jnp.minimum(r + 1, nblk - 1), c))
    variant = lambda b, r: (jnp.where(r == 0, 0, jnp.where(r == nblk - 1, 2, 1)), 0, 0, 0)
    return pl.pallas_call(
        _na_kernel,
        grid=(batch, nblk),
        in_specs=[cur(0), prv(1), cur(1), nxt(1), prv(2), cur(2), nxt(2),
                  _const_spec(tab.shape), pl.BlockSpec((1,) + kaug.shape[1:], variant)],
        out_specs=pl.BlockSpec((nq, NA_WIDTH), lambda b, r: (b * nblk + r, 0)),
        out_shape=jax.ShapeDtypeStruct((n, NA_WIDTH), BF16),
        name="nbr_attn",
        compiler_params=_params("parallel", "parallel"),
    )(qkv, qkv, qkv, qkv, qkv, qkv, qkv, tab, kaug)


def _gla_kernel(qkf_ref, vf_ref, laf_ref, qkb_ref, vb_ref, lab_ref, of_ref, ob_ref, sf_ref, sb_ref):
    c = GLA_CHUNK
    kwid = GLA_KEY_WIDTH
    nchunk = GLA_SUPER // c

    @pl.when(pl.program_id(1) == 0)
    def _():
        sf_ref[...] = jnp.zeros_like(sf_ref)
        sb_ref[...] = jnp.zeros_like(sb_ref)

    ri = lax.broadcasted_iota(I32, (GLA_SUPER, GLA_SUPER), 0)
    ci = lax.broadcasted_iota(I32, (GLA_SUPER, GLA_SUPER), 1)
    same_chunk = (ri // c) == (ci // c)
    tri_f = jnp.logical_and(same_chunk, ci <= ri).astype(BF16)
    tri_b = jnp.logical_and(same_chunk, ci >= ri).astype(BF16)
    ar = lax.broadcasted_iota(I32, (c, kwid), 0)
    ac = lax.broadcasted_iota(I32, (c, kwid), 1) % c
    keep_f = ac <= ar
    keep_b = ac >= ar
    wr_ = lax.broadcasted_iota(I32, (kwid, kwid), 0) // c
    wc_ = lax.broadcasted_iota(I32, (kwid, kwid), 1) // GLA_DK
    wmask = wr_ == wc_
    vr_ = lax.broadcasted_iota(I32, (kwid, GLA_WIDTH), 0) // c
    vc_ = lax.broadcasted_iota(I32, (kwid, GLA_WIDTH), 1) // GLA_DV
    vmask = vr_ == vc_
    sr_ = lax.broadcasted_iota(I32, (GLA_WIDTH, kwid), 0) // GLA_DV
    sc_ = lax.broadcasted_iota(I32, (GLA_WIDTH, kwid), 1) // GLA_DK
    smask = sr_ == sc_

    class Stream:
        def __init__(self, qk_ref, v_ref, la_ref, o_ref, st_ref, tri, keep, last, order):
            self.refs = (qk_ref, v_ref, o_ref, st_ref)
            self.keep, self.last, self.order = keep, last, list(order)
            hi, mid, lo = _split3(la_ref[...])
            self.b_all = _dot(tri, hi) + _dot(tri, mid) + _dot(tri, lo)
            self.qe, self.intra, self.upd, self.dec = {}, {}, {}, {}

        def prepare(self, ic):
            qk_ref, v_ref, _, _ = self.refs
            r = slice(ic * c, (ic + 1) * c)
            b = self.b_all[r]
            bl = b[self.last:self.last + 1, :]
            q = qk_ref[r, :kwid] * (GLA_DK ** -0.5)
            k = qk_ref[r, kwid:]
            v = v_ref[r, :]
            qe = (q * jnp.exp(b)).astype(BF16)
            ke = (k * jnp.exp(-b)).astype(BF16)
            kd = (k * jnp.exp(bl - b)).astype(BF16)
            wt = jnp.where(wmask, jnp.concatenate([ke] * GLA_HEADS, axis=0), jnp.zeros((kwid, kwid), BF16))
            a = jnp.where(self.keep, _dot_nt(qe, wt), 0.0)
            vbd = jnp.where(vmask, jnp.concatenate([v] * GLA_HEADS, axis=0),
                            jnp.zeros((kwid, GLA_WIDTH), BF16))
            self.qe[ic] = qe
            self.intra[ic] = _dot(a.astype(BF16), vbd)
            self.upd[ic] = jnp.where(smask, _dot_tn(v, kd), 0.0)
            self.dec[ic] = jnp.exp(bl)

        def advance(self, pos):
            _, _, o_ref, st_ref = self.refs
            ic = self.order[pos]
            st = st_ref[...] if pos == 0 else self.st
            o_ref[ic * c:(ic + 1) * c, :] = self.intra[ic] + _dot_nt(self.qe[ic], st.astype(BF16))
            self.st = st * self.dec[ic] + self.upd[ic]
            if pos == nchunk - 1:
                st_ref[...] = self.st

    streams = []
    for j in range(GLA_SEQS):
        streams.append(Stream(qkf_ref.at[j], vf_ref.at[j], laf_ref.at[j], of_ref.at[j], sf_ref.at[j],
                              tri_f, keep_f, c - 1, range(nchunk)))
        streams.append(Stream(qkb_ref.at[j], vb_ref.at[j], lab_ref.at[j], ob_ref.at[j], sb_ref.at[j],
                              tri_b, keep_b, 0, range(nchunk - 1, -1, -1)))
    for pos in range(nchunk):
        for s in streams:
            s.prepare(s.order[pos])
    for pos in range(nchunk):
        for s in streams:
            s.advance(pos)


def _gla(qk, v, la, batch, seq):
    n = batch * seq
    ns = seq // GLA_SUPER
    g = GLA_SEQS
    assert seq % GLA_SUPER == 0 and batch % g == 0
    fwd = lambda w, c: pl.BlockSpec((g, GLA_SUPER, w), lambda b, s: (b, s, c))
    bwd = lambda w, c: pl.BlockSpec((g, GLA_SUPER, w), lambda b, s: (b, ns - 1 - s, c))
    kw2 = 2 * GLA_KEY_WIDTH
    seqs = lambda t: t.reshape(batch, seq, t.shape[-1])
    state = pltpu.VMEM((g, GLA_WIDTH, GLA_KEY_WIDTH), F32)
    of, ob = pl.pallas_call(
        _gla_kernel,
        grid=(batch // g, ns),
        in_specs=[fwd(kw2, 0), fwd(GLA_WIDTH, 0), fwd(GLA_KEY_WIDTH, 0),
                  bwd(kw2, 0), bwd(GLA_WIDTH, 0), bwd(GLA_KEY_WIDTH, 1)],
        out_specs=(fwd(GLA_WIDTH, 0), bwd(GLA_WIDTH, 0)),
        out_shape=(jax.ShapeDtypeStruct((batch, seq, GLA_WIDTH), F32),
                   jax.ShapeDtypeStruct((batch, seq, GLA_WIDTH), F32)),
        scratch_shapes=[state, state],
        name="gla",
        compiler_params=_params("parallel", "arbitrary"),
    )(seqs(qk), seqs(v), seqs(la), seqs(qk), seqs(v), seqs(la))
    return of.reshape(n, GLA_WIDTH), ob.reshape(n, GLA_WIDTH)


def _mix_out_kernel(na_ref, of_ref, ob_ref, r_ref, ng_ref, h0_ref, w1_ref, w2_ref, g_ref, b_ref, h1_ref):
    o = of_ref[...] + ob_ref[...]
    parts = []
    for h in range(GLA_HEADS):
        oh = o[:, h * GLA_DV:(h + 1) * GLA_DV]
        ms = jnp.mean(oh * oh, axis=-1, keepdims=True)
        parts.append(oh * lax.rsqrt(ms + RMS_EPS) * ng_ref[...])
    r = r_ref[...]
    gl = jnp.concatenate(parts, axis=-1) * (r * jax.nn.sigmoid(r))
    mixed = _dot(na_ref[...], w1_ref[...]) + _dot(gl.astype(BF16), w2_ref[...])
    h1_ref[...] = _layer_norm(DEEPNORM_ALPHA * h0_ref[...] + mixed, g_ref[...], b_ref[...])


def _mix_out(na, of, ob, r, ng, h0, w1, w2, g, b):
    n = na.shape[0]
    tm = TOK_TILE
    row = lambda w: pl.BlockSpec((tm, w), lambda i: (i, 0))
    return pl.pallas_call(
        _mix_out_kernel,
        grid=(n // tm,),
        in_specs=[row(NA_WIDTH), row(GLA_WIDTH), row(GLA_WIDTH), row(GLA_WIDTH), _const_spec(ng.shape),
                  row(D_MODEL), _const_spec(w1.shape), _const_spec(w2.shape), _const_spec(g.shape),
                  _const_spec(b.shape)],
        out_specs=row(D_MODEL),
        out_shape=jax.ShapeDtypeStruct((n, D_MODEL), F32),
        name="mix_out_ln1",
        compiler_params=_params("parallel"),
    )(na, of, ob, r, ng, h0, w1, w2, g, b)


def _mem_kv_kernel(m_ref, w_ref, o_ref):
    o_ref[...] = _dot(m_ref[...].astype(BF16), w_ref[...]).astype(BF16)


def _mem_kv(mem, wkv):
    rows = mem.shape[0]
    tn = D_MODEL
    return pl.pallas_call(
        _mem_kv_kernel,
        grid=(rows // MEM_TOKENS, wkv.shape[1] // tn),
        in_specs=[pl.BlockSpec((MEM_TOKENS, D_MODEL), lambda i, j: (i, 0)),
                  pl.BlockSpec((D_MODEL, tn), lambda i, j: (0, j))],
        out_specs=pl.BlockSpec((MEM_TOKENS, tn), lambda i, j: (i, j)),
        out_shape=jax.ShapeDtypeStruct((rows, wkv.shape[1]), BF16),
        name="mem_kv",
        compiler_params=_params("parallel", "parallel"),
    )(mem, wkv)


def _xattn_kernel(h1_ref, kv_ref, wq_ref, wo_ref, g_ref, b_ref, wrh_ref, wrl_ref, h2_ref, h2t_ref, aff_ref):
    def project_q(st, rows):
        st["q"] = _dot(h1_ref[rows, :].astype(BF16), wq_ref[...]).astype(BF16)

    def attend(st, rows):
        q = st["q"]

        def scores(h):
            sl = slice(h * MEM_HEAD_DIM, (h + 1) * MEM_HEAD_DIM)
            return _dot_nt(q[:, sl], kv_ref[:, sl]) * (MEM_HEAD_DIM ** -0.5)

        outs = []
        s_nxt = scores(0)
        for h in range(MEM_HEADS):
            s = s_nxt
            if h + 1 < MEM_HEADS:
                s_nxt = scores(h + 1)
            m = jnp.max(s, axis=-1, keepdims=True)
            e = jnp.exp(s - m)
            p = (e / jnp.sum(e, axis=-1, keepdims=True)).astype(BF16)
            outs.append(_dot(p, kv_ref[:, D_MODEL + h * MEM_HEAD_DIM:D_MODEL + (h + 1) * MEM_HEAD_DIM]))
        st["o"] = jnp.concatenate(outs, axis=-1).astype(BF16)

    def project_out(st, rows):
        st["y"] = _dot(st["o"], wo_ref[...])

    def norm(st, rows):
        h2 = _layer_norm(DEEPNORM_ALPHA * h1_ref[rows, :] + st["y"], g_ref[...], b_ref[...])
        h2_ref[rows, :] = h2
        for c in range(D_MODEL // LANES):
            h2t_ref[rows, c, :] = h2[:, c * LANES:(c + 1) * LANES]
        st["h2"] = h2

    def route(st, rows):
        h2 = st["h2"]
        hh = h2.astype(BF16)
        hl = (h2 - hh.astype(F32)).astype(BF16)
        logits = _dot_nt(wrh_ref[...], hh) + _dot_nt(wrh_ref[...], hl) + _dot_nt(wrl_ref[...], hh)
        m = jnp.max(logits, axis=0, keepdims=True)
        e = jnp.exp(logits - m)
        aff_ref[:, rows] = e / jnp.sum(e, axis=0, keepdims=True)

    stages = (project_q, attend, project_out, norm, route)
    tiles = [(dict(), slice(r0, r0 + XATTN_SUB)) for r0 in range(0, TOK_TILE, XATTN_SUB)]
    for k in range(len(stages) + len(tiles) - 1):
        for t, (st, rows) in enumerate(tiles):
            if 0 <= k - t < len(stages):
                stages[k - t](st, rows)


def _xattn(h1, kv, wq, wo, g, b, wrh, wrl, batch, seq):
    n = batch * seq
    tm = TOK_TILE
    nt = seq // tm
    row = pl.BlockSpec((tm, D_MODEL), lambda i: (i, 0))
    return pl.pallas_call(
        _xattn_kernel,
        grid=(n // tm,),
        in_specs=[row, pl.BlockSpec((MEM_TOKENS, 2 * D_MODEL), lambda i: (i // nt, 0)),
                  _const_spec(wq.shape), _const_spec(wo.shape), _const_spec(g.shape), _const_spec(b.shape),
                  _const_spec(wrh.shape), _const_spec(wrl.shape)],
        out_specs=(row, pl.BlockSpec((tm, D_MODEL // LANES, LANES), lambda i: (i, 0, 0)),
                   pl.BlockSpec((N_EXPERTS, tm), lambda i: (0, i))),
        out_shape=(jax.ShapeDtypeStruct((n, D_MODEL), F32),
                   jax.ShapeDtypeStruct((n, D_MODEL // LANES, LANES), F32),
                   jax.ShapeDtypeStruct((N_EXPERTS, n), F32)),
        name="xattn_ln2_router",
        compiler_params=_params("parallel"),
    )(h1, kv, wq, wo, g, b, wrh, wrl)


def _route_kernel(all_ref, aff_ref, idx_ref, gate_ref, pos_ref, bst_ref, thr_ref, dthr_ref, *, cap, pchunk):
    e_n = all_ref.shape[0]

    @pl.when(pl.program_id(0) == 0)
    def _():
        a3 = all_ref[...]

        def counts(x):
            return jnp.sum(jnp.sum(x, axis=2, keepdims=True), axis=1, keepdims=True)

        def largest(n_iter, hi0, admits):
            def step(_, carry):
                lo, hi = carry
                mid = lo + ((hi - lo + 1) >> 1)
                ok = admits(mid)
                return jnp.where(ok, mid, lo), jnp.where(ok, hi, mid - 1)
            init = (jnp.zeros((e_n, 1, 1), I32), jnp.full((e_n, 1, 1), hi0, I32))
            return lax.fori_loop(0, n_iter, step, init)[0]

        tbits = largest(31, 0x7F800000,
                        lambda m: counts((a3 >= lax.bitcast_convert_type(m, F32)).astype(F32)) >= cap)
        thr3 = lax.bitcast_convert_type(tbits, F32)
        d3 = a3 - thr3
        fine = lax.bitcast_convert_type(jnp.maximum((tbits >> 23) - 47, 1) << 23, F32)
        kfine = largest(24, (1 << 24) - 1, lambda m: counts((d3 >= m.astype(F32) * fine).astype(F32)) >= cap)
        thr_ref[...] = jnp.broadcast_to(thr3, thr_ref.shape)
        dthr_ref[...] = jnp.broadcast_to(kfine.astype(F32) * fine, dthr_ref.shape)

    a = aff_ref[0]
    nb = a.shape[0]
    ex = pl.program_id(0)
    d = a - thr_ref[ex][0:1, 0:1]
    dthr = dthr_ref[ex][0:1, 0:1]

    def total(x):
        return jnp.sum(jnp.sum(x, axis=0, keepdims=True), axis=1, keepdims=True)

    li = lax.broadcasted_iota(I32, (LANES, LANES), 0)
    lj = lax.broadcasted_iota(I32, (LANES, LANES), 1)
    upper = (li <= lj).astype(BF16)
    bi = lax.broadcasted_iota(I32, (nb, nb), 0)
    bj = lax.broadcasted_iota(I32, (nb, nb), 1)
    strict = (bj < bi).astype(BF16)
    incl = (bi <= bj).astype(BF16)
    ones8 = jnp.ones((8, LANES), BF16)

    def prefix(mask_b):
        rowcum = _dot(mask_b, upper)
        rowtot = jnp.broadcast_to(rowcum[:, LANES - 1:LANES], (nb, LANES)).astype(BF16)
        return rowcum + _dot(strict, rowtot)

    gt = d > dthr
    eq = d == dthr
    need = cap - total(gt.astype(F32))
    sel = jnp.logical_or(gt, jnp.logical_and(eq, prefix(eq.astype(BF16)) <= need))
    selb = sel.astype(BF16)
    cnt = prefix(selb)

    rt_row = _dot_nt(ones8, selb)
    bend = _dot(rt_row.astype(BF16), incl)
    bstart = bend - rt_row
    pos_ref[0] = jnp.where(sel, cnt - 1.0, -1.0).astype(I32)
    bst_ref[0] = bstart[0:1, :].astype(I32)

    cnt_hi = jnp.floor(cnt * (1.0 / 64.0))
    cnt_lo = cnt - 64.0 * cnt_hi
    a_hi, a_mid, a_lo = _split3(a)
    rowid = lax.broadcasted_iota(I32, (nb, LANES), 0).astype(BF16)
    rhs = jnp.concatenate([cnt_hi.astype(BF16), cnt_lo.astype(BF16), selb, rowid, a_hi, a_mid, a_lo], axis=1)
    lanef = lax.broadcasted_iota(I32, (pchunk, LANES), 1).astype(F32)

    for pc in range(cap // pchunk):
        p = (lax.broadcasted_iota(I32, (pchunk, nb), 0) + pc * pchunk).astype(F32)
        g = jnp.logical_and(bstart[0:1, :] <= p, p < bend[0:1, :]).astype(BF16)
        rows = _dot(g, rhs)
        part = lambda k: rows[:, k * LANES:(k + 1) * LANES]
        crow = part(0) * 64.0 + part(1)
        p1 = (lax.broadcasted_iota(I32, (pchunk, LANES), 0) + (pc * pchunk + 1)).astype(F32)
        oh = jnp.logical_and(crow == p1, part(2) > 0.5)
        pick = lambda x: jnp.where(oh, x, 0.0)
        tok_row = _dot_nt(ones8, pick(part(3)).astype(BF16))
        tok_lane = _dot_nt(ones8, pick(lanef).astype(BF16))
        idx_ref[0, :, pc * pchunk:(pc + 1) * pchunk] = (tok_row[0:1] * float(LANES) + tok_lane[0:1]).astype(I32)
        gate = jnp.sum(pick(part(4) + part(5) + part(6)), axis=1, keepdims=True)
        gate_ref[pc * pchunk:(pc + 1) * pchunk, :] = jnp.broadcast_to(gate, (pchunk, LANES))


def _route(aff_t, cap):
    e, n = aff_t.shape
    nb = n // LANES
    assert nb <= 256 and cap % 64 == 0 and cap // 64 <= 256
    pchunk = min(1024, cap)
    aff3 = aff_t.reshape(e, nb, LANES)
    return pl.pallas_call(
        functools.partial(_route_kernel, cap=cap, pchunk=pchunk),
        grid=(e,),
        in_specs=[_const_spec(aff3.shape), pl.BlockSpec((1, nb, LANES), lambda i: (i, 0, 0))],
        out_specs=(pl.BlockSpec((1, 1, cap), lambda i: (i, 0, 0)),
                   pl.BlockSpec((cap, LANES), lambda i: (i, 0)),
                   pl.BlockSpec((1, nb, LANES), lambda i: (i, 0, 0)),
                   pl.BlockSpec((1, 1, nb), lambda i: (i, 0, 0))),
        out_shape=(jax.ShapeDtypeStruct((e, 1, cap), I32), jax.ShapeDtypeStruct((e * cap, LANES), F32),
                   jax.ShapeDtypeStruct((e, nb, LANES), I32), jax.ShapeDtypeStruct((e, 1, nb), I32)),
        scratch_shapes=[pltpu.VMEM((e, SUBLANES, LANES), F32), pltpu.VMEM((e, SUBLANES, LANES), F32)],
        name="route",
        compiler_params=_params("arbitrary"),
    )(aff3, aff3)


def _ffn_kernel(idx_ref, nxt_ref, nx2_ref, gate_ref, wg_ref, wu_ref, wd_ref, h2_hbm, ye_ref, xbuf, sem):
    s = pl.program_id(0)
    slot = s % FFN_SLOTS
    ahead = (s + FFN_SLOTS - 1) % FFN_SLOTS

    def row_copy(rows_ref, dst_slot, i):
        return pltpu.make_async_copy(h2_hbm.at[rows_ref[0, 0, i]], xbuf.at[dst_slot, i], sem.at[dst_slot])

    def wait_slot(dst_slot):
        pltpu.make_async_copy(h2_hbm.at[pl.ds(0, FFN_ROWS)], xbuf.at[dst_slot], sem.at[dst_slot]).wait()

    @pl.when(s == 0)
    def _():
        def body(i, carry):
            row_copy(idx_ref, 0, i).start()
            row_copy(nxt_ref, 1, i).start()
            return carry
        lax.fori_loop(0, FFN_ROWS, body, 0, unroll=8)

    wait_slot(slot)
    x = jnp.concatenate([xbuf[slot, :, c, :] for c in range(D_MODEL // LANES)], axis=-1).astype(BF16)
    for i in range(FFN_ROWS):
        row_copy(nx2_ref, ahead, i).start(priority=i % 2)
    a = _dot(x, wg_ref[0])
    u = _dot(x, wu_ref[0])
    hmid = (a * jax.nn.sigmoid(a) * u).astype(BF16)
    ye_ref[...] = _dot(hmid, wd_ref[0]) * gate_ref[:, 0:1]

    @pl.when(s + 1 == pl.num_programs(0))
    def _():
        wait_slot((s + 1) % FFN_SLOTS)
        wait_slot(ahead)


def _ffn(idx, gate, wg, wu, wd, h2, cap):
    e = idx.shape[0]
    tr = FFN_ROWS
    per = cap // tr
    nsteps = e * per
    assert cap % tr == 0 and FFN_SLOTS == 3 and nsteps >= FFN_SLOTS
    wspec = lambda shp: pl.BlockSpec((1,) + shp, lambda s: (s // per, 0, 0))
    rows = lambda shift: pl.BlockSpec(
        (1, 1, tr), lambda s: (jnp.minimum(s + shift, nsteps - 1) // per, 0, jnp.minimum(s + shift, nsteps - 1) % per),
        memory_space=pltpu.SMEM)
    return pl.pallas_call(
        _ffn_kernel,
        grid=(nsteps,),
        in_specs=[rows(0), rows(1), rows(2), pl.BlockSpec((tr, LANES), lambda s: (s, 0)),
                  wspec((D_MODEL, D_FF)), wspec((D_MODEL, D_FF)), wspec((D_FF, D_MODEL)),
                  pl.BlockSpec(memory_space=pl.ANY)],
        out_specs=pl.BlockSpec((tr, D_MODEL), lambda s: (s, 0)),
        out_shape=jax.ShapeDtypeStruct((e * cap, D_MODEL), F32),
        scratch_shapes=[pltpu.VMEM((FFN_SLOTS, tr, D_MODEL // LANES, LANES), F32),
                        pltpu.SemaphoreType.DMA((FFN_SLOTS,))],
        name="ffn",
        compiler_params=_params("arbitrary"),
    )(idx, idx, idx, gate, wg, wu, wd, h2)


def _combine_kernel(bst_ref, pos_ref, h2_ref, g_ref, b_ref, ye_hbm, o_ref, ybuf, sem, *, cap):
    e_n = N_EXPERTS
    w = CMB_WIN
    step = w - SUBLANES
    per = CMB_TILE // LANES
    t = pl.program_id(0)
    last = pl.num_programs(0) - 1
    slot = t % 2
    wi = lax.broadcasted_iota(I32, (w, CMB_TILE), 0)

    def windows(tile, r):
        lo = [bst_ref[e, tile * per] + r * step for e in range(e_n)]
        src = [pl.multiple_of(jnp.minimum(e * cap + (lo[e] // SUBLANES) * SUBLANES, e_n * cap - w), SUBLANES)
               for e in range(e_n)]
        return lo, src

    def copies(src, dst_slot):
        return [pltpu.make_async_copy(ye_hbm.at[pl.ds(src[e], w)], ybuf.at[dst_slot, pl.ds(e * w, w)],
                                      sem.at[dst_slot, e]) for e in range(e_n)]

    def contribution(lo, src, buf_slot):
        onehot = []
        for e in range(e_n):
            base = src[e] - e * cap
            p = pos_ref[e:e + 1, :]
            this_round = jnp.logical_and(p >= lo[e], p < lo[e] + step)
            onehot.append(jnp.logical_and(p == base + wi, this_round).astype(BF16))
        oh_t = jnp.concatenate(onehot, axis=0)
        yv = ybuf[buf_slot]
        hi = yv.astype(BF16)
        lo_part = (yv - hi.astype(F32)).astype(BF16)
        return _dot_tn(oh_t, hi) + _dot_tn(oh_t, lo_part)

    lo0, src0 = windows(t, 0)

    @pl.when(t == 0)
    def _():
        for c in copies(src0, 0):
            c.start()

    _, src_next = windows(jnp.minimum(t + 1, last), 0)
    for c in copies(src_next, 1 - slot):
        c.start()
    for c in copies(src0, slot):
        c.wait()
    y = contribution(lo0, src0, slot)

    counts = [bst_ref[e, (t + 1) * per] - bst_ref[e, t * per] for e in range(e_n)]
    nrounds = functools.reduce(jnp.maximum, [(c + (step - 1)) // step for c in counts])

    def extra_round(r, y):
        lo, src = windows(t, r)
        for c in copies(src, slot):
            c.start()
        for c in copies(src, slot):
            c.wait()
        return y + contribution(lo, src, slot)

    y = lax.fori_loop(1, nrounds, extra_round, y)
    o_ref[...] = _layer_norm(DEEPNORM_ALPHA * h2_ref[...] + y, g_ref[...], b_ref[...])

    @pl.when(t == last)
    def _():
        for c in copies(src_next, 1 - slot):
            c.wait()


def _combine(bst, pos, h2, g, b, ye, cap):
    n = h2.shape[0]
    tt = CMB_TILE
    assert cap >= CMB_WIN and n % tt == 0
    grid_spec = pltpu.PrefetchScalarGridSpec(
        num_scalar_prefetch=1,
        grid=(n // tt,),
        in_specs=[pl.BlockSpec((N_EXPERTS, tt), lambda i, *_: (0, i)),
                  pl.BlockSpec((tt, D_MODEL), lambda i, *_: (i, 0)),
                  pl.BlockSpec(g.shape, lambda i, *_: (0, 0)), pl.BlockSpec(b.shape, lambda i, *_: (0, 0)),
                  pl.BlockSpec(memory_space=pl.ANY)],
        out_specs=pl.BlockSpec((tt, D_MODEL), lambda i, *_: (i, 0)),
        scratch_shapes=[pltpu.VMEM((2, N_EXPERTS * CMB_WIN, D_MODEL), F32),
                        pltpu.SemaphoreType.DMA((2, N_EXPERTS))],
    )
    return pl.pallas_call(
        functools.partial(_combine_kernel, cap=cap),
        grid_spec=grid_spec,
        out_shape=jax.ShapeDtypeStruct((n, D_MODEL), F32),
        name="combine_ln3",
        compiler_params=_params("arbitrary"),
    )(bst, pos, h2, g, b, ye)


def _prepare(ln_in_g, ln_in_b, w_in, na_rpb, gla_gate_w2, gla_gate_b, gla_norm_g, w_out, ln1_g, ln1_b,
             mem_wq, mem_wkv, mem_wo, ln2_g, ln2_b, w_router, w_gate, w_up, w_down, ln3_g, ln3_b):
    row = lambda v: v.reshape(1, -1).astype(F32)
    w = w_in[0]
    o = np.cumsum((0, NA_WIDTH, NA_WIDTH, NA_WIDTH, GLA_KEY_WIDTH, GLA_KEY_WIDTH, GLA_WIDTH, GLA_WIDTH,
                   2 * GLA_GATE_RANK))
    wgk = jnp.zeros((D_MODEL, LANES), F32).at[:, :2 * GLA_GATE_RANK].set(w[:, o[7]:o[8]])
    w2 = jnp.zeros((LANES, 2 * GLA_KEY_WIDTH), F32)
    for s in range(2):
        w2 = w2.at[s * GLA_GATE_RANK:(s + 1) * GLA_GATE_RANK,
                   s * GLA_KEY_WIDTH:(s + 1) * GLA_KEY_WIDTH].set(gla_gate_w2[0, s])
    tab, kaug = _na_tables(na_rpb[0])
    wr_t = w_router[0].T.astype(F32)
    wr_hi = wr_t.astype(BF16)
    wr_lo = (wr_t - wr_hi.astype(F32)).astype(BF16)
    return dict(
        ln_in=(row(ln_in_g), row(ln_in_b)),
        wna=w[:, o[0]:o[3]].astype(BF16), wqk=w[:, o[3]:o[5]].astype(BF16), wv=w[:, o[5]:o[6]].astype(BF16),
        wr=w[:, o[6]:o[7]].astype(BF16), wgk=wgk.astype(BF16), w2=w2.astype(BF16),
        gb=gla_gate_b[0].reshape(1, -1).astype(F32),
        tab=tab, kaug=kaug, ng=row(gla_norm_g[0]),
        wo1=w_out[0][:NA_WIDTH].astype(BF16), wo2=w_out[0][NA_WIDTH:].astype(BF16),
        ln1=(row(ln1_g[0]), row(ln1_b[0])),
        wq=mem_wq[0].astype(BF16), wkv=mem_wkv[0].astype(BF16), wmo=mem_wo[0].astype(BF16),
        ln2=(row(ln2_g[0]), row(ln2_b[0])),
        wr_hi=wr_hi, wr_lo=wr_lo,
        wg=w_gate[0].astype(BF16), wu=w_up[0].astype(BF16), wd=w_down[0].astype(BF16),
        ln3=(row(ln3_g[0]), row(ln3_b[0])),
    )


def _trunk(x, mem, p):
    batch, seq, _ = x.shape
    n = batch * seq
    cap = EC_CAPACITY_FACTOR * n // N_EXPERTS
    h0, qkv, qk, v, r, la = _in_proj(x.reshape(n, D_MODEL), *p["ln_in"], p["wna"], p["wqk"], p["wv"], p["wr"],
                                     p["wgk"], p["w2"], p["gb"])
    na = _na(qkv, p["tab"], p["kaug"], batch, seq)
    of, ob = _gla(qk, v, la, batch, seq)
    h1 = _mix_out(na, of, ob, r, p["ng"], h0, p["wo1"], p["wo2"], *p["ln1"])
    kv = _mem_kv(mem.reshape(batch * MEM_TOKENS, D_MODEL), p["wkv"])
    h2, h2t, aff_t = _xattn(h1, kv, p["wq"], p["wmo"], *p["ln2"], p["wr_hi"], p["wr_lo"], batch, seq)
    idx, gate, pos, bst = _route(aff_t, cap)
    ye = _ffn(idx, gate, p["wg"], p["wu"], p["wd"], h2t, cap)
    bst = jnp.concatenate([bst[:, 0, :], jnp.full((N_EXPERTS, 1), cap, I32)], axis=1)
    out = _combine(bst, pos.reshape(N_EXPERTS, n), h2, *p["ln3"], ye, cap)
    return out.reshape(batch, seq, D_MODEL)


def kernel(x_prompt, x_sample, mem_prompt, mem_sample, ln_in_g, ln_in_b, w_in, na_rpb, gla_gate_w2, gla_gate_b,
           gla_norm_g, w_out, ln1_g, ln1_b, mem_wq, mem_wkv, mem_wo, ln2_g, ln2_b, w_router, w_gate, w_up, w_down,
           ln3_g, ln3_b):
    p = _prepare(ln_in_g, ln_in_b, w_in, na_rpb, gla_gate_w2, gla_gate_b, gla_norm_g, w_out, ln1_g, ln1_b,
                 mem_wq, mem_wkv, mem_wo, ln2_g, ln2_b, w_router, w_gate, w_up, w_down, ln3_g, ln3_b)
    return _trunk(x_prompt, mem_prompt, p), _trunk(x_sample, mem_sample, p)
```

```python
import functools

import numpy as np
import jax
import jax.numpy as jnp
from jax import lax
from jax.experimental import pallas as pl
from jax.experimental.pallas import tpu as pltpu

F32 = jnp.float32
BF16 = jnp.bfloat16
I32 = jnp.int32

D_MODEL = 1024
GRID_W = 64
NA_HEADS = 8
NA_HEAD_DIM = 64
NA_WIDTH = NA_HEADS * NA_HEAD_DIM
NA_WIN_ROWS = 8
NA_WIN_COLS = 16
GLA_HEADS = 4
GLA_DK = 64
GLA_DV = 128
GLA_KEY_WIDTH = GLA_HEADS * GLA_DK
GLA_WIDTH = GLA_HEADS * GLA_DV
GLA_GATE_RANK = 16
GLA_GATE_NORM = 16.0
GLA_CHUNK = 64
MEM_TOKENS = 256
MEM_HEADS = 4
MEM_HEAD_DIM = D_MODEL // MEM_HEADS
N_EXPERTS = 16
EC_CAPACITY_FACTOR = 2
D_FF = 2 * D_MODEL
LN_EPS = 1e-5
RMS_EPS = 1e-5
DEPTH = 1
DEEPNORM_ALPHA = (2 * DEPTH) ** 0.25

LANES = 128
SUBLANES = 8
NEG = -1e30
VMEM_LIMIT = 56 * 1024 * 1024

NA_QROWS = 8
NA_SLAB = 16
NA_SUB = 4
NA_AHEAD = 1
GLA_SUPER = 512
GLA_SEQS = 1
TOK_TILE = 512
SUB_TILE = 256
XATTN_SUB = 512
FFN_ROWS = 512
FFN_SLOTS = 3
CMB_TILE = 256
CMB_WIN = 64


def _dot(a, b):
    return jnp.dot(a, b, preferred_element_type=F32)


def _dot_nt(a, b):
    return lax.dot_general(a, b, (((1,), (1,)), ((), ())), preferred_element_type=F32)


def _dot_tn(a, b):
    return lax.dot_general(a, b, (((0,), (0,)), ((), ())), preferred_element_type=F32)


def _layer_norm(x, g, b):
    mu = jnp.mean(x, axis=-1, keepdims=True)
    xc = x - mu
    var = jnp.mean(xc * xc, axis=-1, keepdims=True)
    return xc * lax.rsqrt(var + LN_EPS) * g + b


def _split3(x):
    hi = x.astype(BF16)
    r1 = x - hi.astype(F32)
    mid = r1.astype(BF16)
    lo = (r1 - mid.astype(F32)).astype(BF16)
    return hi, mid, lo


def _params(*sem):
    return pltpu.CompilerParams(dimension_semantics=sem, vmem_limit_bytes=VMEM_LIMIT)


def _const_spec(shape):
    nd = len(shape)
    return pl.BlockSpec(shape, lambda *_: (0,) * nd)


def _in_proj_kernel(x_ref, g_ref, b_ref, wna_ref, wqk_ref, wv_ref, wr_ref, wgk_ref, w2_ref, gb_ref,
                    h0_ref, qkv_ref, qk_ref, v_ref, r_ref, la_ref):
    def norm(st, rows):
        h0 = _layer_norm(x_ref[rows, :], g_ref[...], b_ref[...])
        h0_ref[rows, :] = h0
        st["hb"] = h0.astype(BF16)

    def project(st, rows):
        hb = st["hb"]
        qkv_ref[rows, :] = _dot(hb, wna_ref[...]).astype(BF16)
        qk_ref[rows, :] = _dot(hb, wqk_ref[...])
        v_ref[rows, :] = _dot(hb, wv_ref[...]).astype(BF16)
        r_ref[rows, :] = _dot(hb, wr_ref[...])
        st["gk"] = _dot(hb, wgk_ref[...])

    def decay_gate(st, rows):
        z = _dot(st["gk"].astype(BF16), w2_ref[...]) + gb_ref[...]
        log_sig = jnp.minimum(z, 0.0) - jnp.log1p(jnp.exp(-jnp.abs(z)))
        la_ref[rows, :] = log_sig * (1.0 / GLA_GATE_NORM)

    stages = (norm, project, decay_gate)
    tiles = [(dict(), slice(r0, r0 + SUB_TILE)) for r0 in range(0, TOK_TILE, SUB_TILE)]
    for k in range(len(stages) + len(tiles) - 1):
        for t, (st, rows) in enumerate(tiles):
            if 0 <= k - t < len(stages):
                stages[k - t](st, rows)


def _in_proj(x, g, b, wna, wqk, wv, wr, wgk, w2, gb):
    n = x.shape[0]
    tm = TOK_TILE
    row = lambda w: pl.BlockSpec((tm, w), lambda i: (i, 0))
    outs = (
        jax.ShapeDtypeStruct((n, D_MODEL), F32),
        jax.ShapeDtypeStruct((n, 3 * NA_WIDTH), BF16),
        jax.ShapeDtypeStruct((n, 2 * GLA_KEY_WIDTH), F32),
        jax.ShapeDtypeStruct((n, GLA_WIDTH), BF16),
        jax.ShapeDtypeStruct((n, GLA_WIDTH), F32),
        jax.ShapeDtypeStruct((n, 2 * GLA_KEY_WIDTH), F32),
    )
    return pl.pallas_call(
        _in_proj_kernel,
        grid=(n // tm,),
        in_specs=[row(D_MODEL), _const_spec(g.shape), _const_spec(b.shape), _const_spec(wna.shape),
                  _const_spec(wqk.shape), _const_spec(wv.shape), _const_spec(wr.shape),
                  _const_spec(wgk.shape), _const_spec(w2.shape), _const_spec(gb.shape)],
        out_specs=(row(D_MODEL), row(3 * NA_WIDTH), row(2 * GLA_KEY_WIDTH), row(GLA_WIDTH), row(GLA_WIDTH),
                   row(2 * GLA_KEY_WIDTH)),
        out_shape=outs,
        name="ln_in_proj",
        compiler_params=_params("parallel"),
    )(x, g, b, wna, wqk, wv, wr, wgk, w2, gb)


def _na_tables(rpb):
    kw = NA_WIN_COLS
    cols = np.arange(GRID_W)
    col_start = np.clip(cols - kw // 2, 0, GRID_W - kw)
    j = np.arange(GRID_W)
    col_ok = (j[None, :] >= col_start[:, None]) & (j[None, :] < col_start[:, None] + kw)
    col_off = np.clip(j[None, :] - cols[:, None] + (NA_WIN_COLS - 1), 0, 2 * NA_WIN_COLS - 2)
    nblk = NA_SLAB + NA_QROWS
    ro = np.arange(nblk) - 5
    ro_ok = (ro >= 0) & (ro < 2 * NA_WIN_ROWS - 1)
    ro_c = np.clip(ro, 0, 2 * NA_WIN_ROWS - 2)
    t = rpb[:, ro_c][:, :, col_off]
    ok = jnp.asarray(ro_ok[None, :, None, None] & col_ok[None, None])
    t = jnp.where(ok, t, NEG).astype(F32)
    flat = jnp.transpose(t, (0, 2, 1, 3)).reshape(NA_HEADS, GRID_W, nblk * GRID_W)
    shifted = jnp.concatenate([flat[:, :, GRID_W:], jnp.full((NA_HEADS, GRID_W, GRID_W), NEG, F32)], axis=-1)
    tab = jnp.stack([flat, shifted])

    a = np.arange(NA_QROWS)
    i = np.arange(NA_SLAB)
    kaug = np.zeros((3, 2, NA_SLAB * GRID_W, LANES), np.float32)
    for vi, w in enumerate((np.maximum(a, 4), a, np.minimum(a, 4))):
        valid = (i[None, :] >= w[:, None]) & (i[None, :] < w[:, None] + NA_WIN_ROWS)
        per_key = np.repeat(np.where(valid, 0.0, NEG).T, GRID_W, axis=0)
        kaug[vi, 0, :, NA_HEAD_DIM:NA_HEAD_DIM + NA_QROWS] = per_key
        kaug[vi, 1, :, :NA_QROWS] = per_key
    return tab, jnp.asarray(kaug).astype(BF16)


def _na_kernel(q_ref, kp_ref, kc_ref, kn_ref, vp_ref, vc_ref, vn_ref, tab_ref, kaug_ref, o_ref):
    half = NA_QROWS * GRID_W // 2
    k = jnp.concatenate([kp_ref[half:, :], kc_ref[...], kn_ref[:half, :]], axis=0)
    v = jnp.concatenate([vp_ref[half:, :], vc_ref[...], vn_ref[:half, :]], axis=0)
    nq = NA_SUB * GRID_W
    nk = (NA_SUB + NA_WIN_ROWS) * GRID_W
    qlane = lax.broadcasted_iota(I32, (nq, LANES), 1)
    qrow = lax.broadcasted_iota(I32, (nq, LANES), 0) // GRID_W
    klane = lax.broadcasted_iota(I32, (nk, LANES), 1)
    first = qlane < NA_HEAD_DIM

    def scores(hp, u, sub):
        sl = slice(hp * LANES, (hp + 1) * LANES)
        r0 = u * nq
        h = 2 * hp + sub
        q2 = q_ref[r0:r0 + nq, sl] * (NA_HEAD_DIM ** -0.5)
        aug0 = NA_HEAD_DIM if sub == 0 else 0
        onehot = (qlane == aug0 + u * NA_SUB + qrow).astype(BF16)
        qm = jnp.where(first if sub == 0 else jnp.logical_not(first), q2, onehot)
        in_aug = jnp.logical_and(klane >= aug0, klane < aug0 + NA_QROWS)
        km = jnp.where(in_aug, kaug_ref[0, sub, r0:r0 + nk, :], k[r0:r0 + nk, sl])
        s = _dot_nt(qm, km)
        strips = []
        for al in range(NA_SUB):
            a = u * NA_SUB + al
            par = a % 2
            off = ((8 - a) if par == 0 else (7 - a)) * GRID_W + r0
            strips.append(s[al * GRID_W:(al + 1) * GRID_W, :] + tab_ref[par, h, :, off:off + nk])
        return jnp.concatenate(strips, axis=0)

    def attend(s, hp, u):
        m = jnp.max(s, axis=-1, keepdims=True)
        e = jnp.exp(s - m)
        l = jnp.sum(e, axis=-1, keepdims=True)
        return _dot(e.astype(BF16), v[u * nq:u * nq + nk, hp * LANES:(hp + 1) * LANES]) / l

    units = [(hp, u, sub) for hp in range(NA_HEADS // 2) for u in range(NA_QROWS // NA_SUB) for sub in range(2)]
    pending = [scores(*units[j]) for j in range(NA_AHEAD)]
    pv = {}
    for i, (hp, u, sub) in enumerate(units):
        if i + NA_AHEAD < len(units):
            pending.append(scores(*units[i + NA_AHEAD]))
        pv[sub] = attend(pending.pop(0), hp, u)
        if sub == 1:
            o_ref[u * nq:(u + 1) * nq, hp * LANES:(hp + 1) * LANES] = jnp.where(first, pv[0], pv[1]).astype(BF16)


def _na(qkv, tab, kaug, batch, seq):
    n = batch * seq
    nq = NA_QROWS * GRID_W
    nblk = seq // nq
    assert seq % nq == 0 and nblk >= 2
    cur = lambda c: pl.BlockSpec((nq, NA_WIDTH), lambda b, r: (b * nblk + r, c))
    prv = lambda c: pl.BlockSpec((nq, NA_WIDTH), lambda b, r: (b * nblk + jnp.maximum(r - 1, 0), c))
    nxt = lambda c: pl.BlockSpec((nq, NA_WIDTH), lambda b, r: (b * nblk + jnp.minimum(r + 1, nblk - 1), c))
    variant = lambda b, r: (jnp.where(r == 0, 0, jnp.where(r == nblk - 1, 2, 1)), 0, 0, 0)
    return pl.pallas_call(
        _na_kernel,
        grid=(batch, nblk),
        in_specs=[cur(0), prv(1), cur(1), nxt(1), prv(2), cur(2), nxt(2),
                  _const_spec(tab.shape), pl.BlockSpec((1,) + kaug.shape[1:], variant)],
        out_specs=pl.BlockSpec((nq, NA_WIDTH), lambda b, r: (b * nblk + r, 0)),
        out_shape=jax.ShapeDtypeStruct((n, NA_WIDTH), BF16),
        name="nbr_attn",
        compiler_params=_params("parallel", "parallel"),
    )(qkv, qkv, qkv, qkv, qkv, qkv, qkv, tab, kaug)


def _gla_kernel(qkf_ref, vf_ref, laf_ref, qkb_ref, vb_ref, lab_ref, of_ref, ob_ref, sf_ref, sb_ref):
    c = GLA_CHUNK
    kwid = GLA_KEY_WIDTH
    nchunk = GLA_SUPER // c

    @pl.when(pl.program_id(1) == 0)
    def _():
        sf_ref[...] = jnp.zeros_like(sf_ref)
        sb_ref[...] = jnp.zeros_like(sb_ref)

    ri = lax.broadcasted_iota(I32, (GLA_SUPER, GLA_SUPER), 0)
    ci = lax.broadcasted_iota(I32, (GLA_SUPER, GLA_SUPER), 1)
    same_chunk = (ri // c) == (ci // c)
    tri_f = jnp.logical_and(same_chunk, ci <= ri).astype(BF16)
    tri_b = jnp.logical_and(same_chunk, ci >= ri).astype(BF16)
    ar = lax.broadcasted_iota(I32, (c, kwid), 0)
    ac = lax.broadcasted_iota(I32, (c, kwid), 1) % c
    keep_f = ac <= ar
    keep_b = ac >= ar
    wr_ = lax.broadcasted_iota(I32, (kwid, kwid), 0) // c
    wc_ = lax.broadcasted_iota(I32, (kwid, kwid), 1) // GLA_DK
    wmask = wr_ == wc_
    vr_ = lax.broadcasted_iota(I32, (kwid, GLA_WIDTH), 0) // c
    vc_ = lax.broadcasted_iota(I32, (kwid, GLA_WIDTH), 1) // GLA_DV
    vmask = vr_ == vc_
    sr_ = lax.broadcasted_iota(I32, (GLA_WIDTH, kwid), 0) // GLA_DV
    sc_ = lax.broadcasted_iota(I32, (GLA_WIDTH, kwid), 1) // GLA_DK
    smask = sr_ == sc_

    class Stream:
        def __init__(self, qk_ref, v_ref, la_ref, o_ref, st_ref, tri, keep, last, order):
            self.refs = (qk_ref, v_ref, o_ref, st_ref)
            self.keep, self.last, self.order = keep, last, list(order)
            hi, mid, lo = _split3(la_ref[...])
            self.b_all = _dot(tri, hi) + _dot(tri, mid) + _dot(tri, lo)
            self.qe, self.intra, self.upd, self.dec = {}, {}, {}, {}

        def prepare(self, ic):
            qk_ref, v_ref, _, _ = self.refs
            r = slice(ic * c, (ic + 1) * c)
            b = self.b_all[r]
            bl = b[self.last:self.last + 1, :]
            q = qk_ref[r, :kwid] * (GLA_DK ** -0.5)
            k = qk_ref[r, kwid:]
            v = v_ref[r, :]
            qe = (q * jnp.exp(b)).astype(BF16)
            ke = (k * jnp.exp(-b)).astype(BF16)
            kd = (k * jnp.exp(bl - b)).astype(BF16)
            wt = jnp.where(wmask, jnp.concatenate([ke] * GLA_HEADS, axis=0), jnp.zeros((kwid, kwid), BF16))
            a = jnp.where(self.keep, _dot_nt(qe, wt), 0.0)
            vbd = jnp.where(vmask, jnp.concatenate([v] * GLA_HEADS, axis=0),
                            jnp.zeros((kwid, GLA_WIDTH), BF16))
            self.qe[ic] = qe
            self.intra[ic] = _dot(a.astype(BF16), vbd)
            self.upd[ic] = jnp.where(smask, _dot_tn(v, kd), 0.0)
            self.dec[ic] = jnp.exp(bl)

        def advance(self, pos):
            _, _, o_ref, st_ref = self.refs
            ic = self.order[pos]
            st = st_ref[...] if pos == 0 else self.st
            o_ref[ic * c:(ic + 1) * c, :] = self.intra[ic] + _dot_nt(self.qe[ic], st.astype(BF16))
            self.st = st * self.dec[ic] + self.upd[ic]
            if pos == nchunk - 1:
                st_ref[...] = self.st

    streams = []
    for j in range(GLA_SEQS):
        streams.append(Stream(qkf_ref.at[j], vf_ref.at[j], laf_ref.at[j], of_ref.at[j], sf_ref.at[j],
                              tri_f, keep_f, c - 1, range(nchunk)))
        streams.append(Stream(qkb_ref.at[j], vb_ref.at[j], lab_ref.at[j], ob_ref.at[j], sb_ref.at[j],
                              tri_b, keep_b, 0, range(nchunk - 1, -1, -1)))
    for pos in range(nchunk):
        for s in streams:
            s.prepare(s.order[pos])
    for pos in range(nchunk):
        for s in streams:
            s.advance(pos)


def _gla(qk, v, la, batch, seq):
    n = batch * seq
    ns = seq // GLA_SUPER
    g = GLA_SEQS
    assert seq % GLA_SUPER == 0 and batch % g == 0
    fwd = lambda w, c: pl.BlockSpec((g, GLA_SUPER, w), lambda b, s: (b, s, c))
    bwd = lambda w, c: pl.BlockSpec((g, GLA_SUPER, w), lambda b, s: (b, ns - 1 - s, c))
    kw2 = 2 * GLA_KEY_WIDTH
    seqs = lambda t: t.reshape(batch, seq, t.shape[-1])
    state = pltpu.VMEM((g, GLA_WIDTH, GLA_KEY_WIDTH), F32)
    of, ob = pl.pallas_call(
        _gla_kernel,
        grid=(batch // g, ns),
        in_specs=[fwd(kw2, 0), fwd(GLA_WIDTH, 0), fwd(GLA_KEY_WIDTH, 0),
                  bwd(kw2, 0), bwd(GLA_WIDTH, 0), bwd(GLA_KEY_WIDTH, 1)],
        out_specs=(fwd(GLA_WIDTH, 0), bwd(GLA_WIDTH, 0)),
        out_shape=(jax.ShapeDtypeStruct((batch, seq, GLA_WIDTH), F32),
                   jax.ShapeDtypeStruct((batch, seq, GLA_WIDTH), F32)),
        scratch_shapes=[state, state],
        name="gla",
        compiler_params=_params("parallel", "arbitrary"),
    )(seqs(qk), seqs(v), seqs(la), seqs(qk), seqs(v), seqs(la))
    return of.reshape(n, GLA_WIDTH), ob.reshape(n, GLA_WIDTH)


def _mem_kv_kernel(m_ref, w_ref, o_ref):
    o_ref[...] = _dot(m_ref[...].astype(BF16), w_ref[...]).astype(BF16)


def _mem_kv(mem, wkv):
    rows = mem.shape[0]
    tn = D_MODEL
    return pl.pallas_call(
        _mem_kv_kernel,
        grid=(rows // MEM_TOKENS, wkv.shape[1] // tn),
        in_specs=[pl.BlockSpec((MEM_TOKENS, D_MODEL), lambda i, j: (i, 0)),
                  pl.BlockSpec((D_MODEL, tn), lambda i, j: (0, j))],
        out_specs=pl.BlockSpec((MEM_TOKENS, tn), lambda i, j: (i, j)),
        out_shape=jax.ShapeDtypeStruct((rows, wkv.shape[1]), BF16),
        name="mem_kv",
        compiler_params=_params("parallel", "parallel"),
    )(mem, wkv)


def _xattn_kernel(na_ref, of_ref, ob_ref, r_ref, ng_ref, h0_ref, w1_ref, w2_ref, g1_ref, b1_ref,
                  kv_ref, wq_ref, wo_ref, g_ref, b_ref, wrh_ref, wrl_ref, h2_ref, h2t_ref, aff_ref):
    def mix_out(st, rows):
        o = of_ref[rows, :] + ob_ref[rows, :]
        parts = []
        for h in range(GLA_HEADS):
            oh = o[:, h * GLA_DV:(h + 1) * GLA_DV]
            ms = jnp.mean(oh * oh, axis=-1, keepdims=True)
            parts.append(oh * lax.rsqrt(ms + RMS_EPS) * ng_ref[...])
        r = r_ref[rows, :]
        gl = jnp.concatenate(parts, axis=-1) * (r * jax.nn.sigmoid(r))
        mixed = _dot(na_ref[rows, :], w1_ref[...]) + _dot(gl.astype(BF16), w2_ref[...])
        st["h1"] = _layer_norm(DEEPNORM_ALPHA * h0_ref[rows, :] + mixed, g1_ref[...], b1_ref[...])

    def project_q(st, rows):
        st["q"] = _dot(st["h1"].astype(BF16), wq_ref[...]).astype(BF16)

    def attend(st, rows):
        q = st["q"]

        def scores(h):
            sl = slice(h * MEM_HEAD_DIM, (h + 1) * MEM_HEAD_DIM)
            return _dot_nt(q[:, sl], kv_ref[:, sl]) * (MEM_HEAD_DIM ** -0.5)

        outs = []
        s_nxt = scores(0)
        for h in range(MEM_HEADS):
            s = s_nxt
            if h + 1 < MEM_HEADS:
                s_nxt = scores(h + 1)
            m = jnp.max(s, axis=-1, keepdims=True)
            e = jnp.exp(s - m)
            p = (e / jnp.sum(e, axis=-1, keepdims=True)).astype(BF16)
            outs.append(_dot(p, kv_ref[:, D_MODEL + h * MEM_HEAD_DIM:D_MODEL + (h + 1) * MEM_HEAD_DIM]))
        st["o"] = jnp.concatenate(outs, axis=-1).astype(BF16)

    def project_out(st, rows):
        st["y"] = _dot(st["o"], wo_ref[...])

    def norm(st, rows):
        st["h2"] = _layer_norm(DEEPNORM_ALPHA * st["h1"] + st["y"], g_ref[...], b_ref[...])

    def route(st, rows):
        h2 = st["h2"]
        hh = h2.astype(BF16)
        hl = (h2 - hh.astype(F32)).astype(BF16)
        logits = _dot_nt(wrh_ref[...], hh) + _dot_nt(wrh_ref[...], hl) + _dot_nt(wrl_ref[...], hh)
        h2_ref[rows, :] = h2
        for c in range(D_MODEL // LANES):
            h2t_ref[rows, c, :] = h2[:, c * LANES:(c + 1) * LANES]
        m = jnp.max(logits, axis=0, keepdims=True)
        e = jnp.exp(logits - m)
        aff_ref[:, rows] = e / jnp.sum(e, axis=0, keepdims=True)

    stages = (mix_out, project_q, attend, project_out, norm, route)
    tiles = [(dict(), slice(r0, r0 + XATTN_SUB)) for r0 in range(0, TOK_TILE, XATTN_SUB)]
    for k in range(len(stages) + len(tiles) - 1):
        for t, (st, rows) in enumerate(tiles):
            if 0 <= k - t < len(stages):
                stages[k - t](st, rows)


def _xattn(na, of, ob, r, ng, h0, w1, w2, g1, b1, kv, wq, wo, g, b, wrh, wrl, batch, seq):
    n = batch * seq
    tm = TOK_TILE
    nt = seq // tm
    rows = lambda w: pl.BlockSpec((tm, w), lambda i: (i, 0))
    row = rows(D_MODEL)
    consts = lambda *ts: [_const_spec(t.shape) for t in ts]
    return pl.pallas_call(
        _xattn_kernel,
        grid=(n // tm,),
        in_specs=[rows(NA_WIDTH), rows(GLA_WIDTH), rows(GLA_WIDTH), rows(GLA_WIDTH), *consts(ng), row,
                  *consts(w1, w2, g1, b1),
                  pl.BlockSpec((MEM_TOKENS, 2 * D_MODEL), lambda i: (i // nt, 0)),
                  *consts(wq, wo, g, b, wrh, wrl)],
        out_specs=(row, pl.BlockSpec((tm, D_MODEL // LANES, LANES), lambda i: (i, 0, 0)),
                   pl.BlockSpec((N_EXPERTS, tm), lambda i: (0, i))),
        out_shape=(jax.ShapeDtypeStruct((n, D_MODEL), F32),
                   jax.ShapeDtypeStruct((n, D_MODEL // LANES, LANES), F32),
                   jax.ShapeDtypeStruct((N_EXPERTS, n), F32)),
        name="mix_xattn_router",
        compiler_params=_params("parallel"),
    )(na, of, ob, r, ng, h0, w1, w2, g1, b1, kv, wq, wo, g, b, wrh, wrl)


def _route_kernel(all_ref, aff_ref, idx_ref, gate_ref, pos_ref, bst_ref, thr_ref, dthr_ref, *, cap, pchunk):
    e_n = all_ref.shape[0]

    @pl.when(pl.program_id(0) == 0)
    def _():
        a3 = all_ref[...]

        def counts(x):
            return jnp.sum(jnp.sum(x, axis=2, keepdims=True), axis=1, keepdims=True)

        def largest(n_iter, hi0, admits):
            def step(_, carry):
                lo, hi = carry
                mid = lo + ((hi - lo + 1) >> 1)
                ok = admits(mid)
                return jnp.where(ok, mid, lo), jnp.where(ok, hi, mid - 1)
            init = (jnp.zeros((e_n, 1, 1), I32), jnp.full((e_n, 1, 1), hi0, I32))
            return lax.fori_loop(0, n_iter, step, init)[0]

        tbits = largest(31, 0x7F800000,
                        lambda m: counts((a3 >= lax.bitcast_convert_type(m, F32)).astype(F32)) >= cap)
        thr3 = lax.bitcast_convert_type(tbits, F32)
        d3 = a3 - thr3
        fine = lax.bitcast_convert_type(jnp.maximum((tbits >> 23) - 47, 1) << 23, F32)
        kfine = largest(24, (1 << 24) - 1, lambda m: counts((d3 >= m.astype(F32) * fine).astype(F32)) >= cap)
        thr_ref[...] = jnp.broadcast_to(thr3, thr_ref.shape)
        dthr_ref[...] = jnp.broadcast_to(kfine.astype(F32) * fine, dthr_ref.shape)

    a = aff_ref[0]
    nb = a.shape[0]
    ex = pl.program_id(0)
    d = a - thr_ref[ex][0:1, 0:1]
    dthr = dthr_ref[ex][0:1, 0:1]

    def total(x):
        return jnp.sum(jnp.sum(x, axis=0, keepdims=True), axis=1, keepdims=True)

    li = lax.broadcasted_iota(I32, (LANES, LANES), 0)
    lj = lax.broadcasted_iota(I32, (LANES, LANES), 1)
    upper = (li <= lj).astype(BF16)
    bi = lax.broadcasted_iota(I32, (nb, nb), 0)
    bj = lax.broadcasted_iota(I32, (nb, nb), 1)
    strict = (bj < bi).astype(BF16)
    incl = (bi <= bj).astype(BF16)
    ones8 = jnp.ones((8, LANES), BF16)

    def prefix(mask_b):
        rowcum = _dot(mask_b, upper)
        rowtot = jnp.broadcast_to(rowcum[:, LANES - 1:LANES], (nb, LANES)).astype(BF16)
        return rowcum + _dot(strict, rowtot)

    gt = d > dthr
    eq = d == dthr
    need = cap - total(gt.astype(F32))
    sel = jnp.logical_or(gt, jnp.logical_and(eq, prefix(eq.astype(BF16)) <= need))
    selb = sel.astype(BF16)
    cnt = prefix(selb)

    rt_row = _dot_nt(ones8, selb)
    bend = _dot(rt_row.astype(BF16), incl)
    bstart = bend - rt_row
    pos_ref[0] = jnp.where(sel, cnt - 1.0, -1.0).astype(I32)
    bst_ref[0] = bstart[0:1, :].astype(I32)

    cnt_hi = jnp.floor(cnt * (1.0 / 64.0))
    cnt_lo = cnt - 64.0 * cnt_hi
    a_hi, a_mid, a_lo = _split3(a)
    rowid = lax.broadcasted_iota(I32, (nb, LANES), 0).astype(BF16)
    rhs = jnp.concatenate([cnt_hi.astype(BF16), cnt_lo.astype(BF16), selb, rowid, a_hi, a_mid, a_lo], axis=1)
    lanef = lax.broadcasted_iota(I32, (pchunk, LANES), 1).astype(F32)

    for pc in range(cap // pchunk):
        p = (lax.broadcasted_iota(I32, (pchunk, nb), 0) + pc * pchunk).astype(F32)
        g = jnp.logical_and(bstart[0:1, :] <= p, p < bend[0:1, :]).astype(BF16)
        rows = _dot(g, rhs)
        part = lambda k: rows[:, k * LANES:(k + 1) * LANES]
        crow = part(0) * 64.0 + part(1)
        p1 = (lax.broadcasted_iota(I32, (pchunk, LANES), 0) + (pc * pchunk + 1)).astype(F32)
        oh = jnp.logical_and(crow == p1, part(2) > 0.5)
        pick = lambda x: jnp.where(oh, x, 0.0)
        tok_row = _dot_nt(ones8, pick(part(3)).astype(BF16))
        tok_lane = _dot_nt(ones8, pick(lanef).astype(BF16))
        idx_ref[0, :, pc * pchunk:(pc + 1) * pchunk] = (tok_row[0:1] * float(LANES) + tok_lane[0:1]).astype(I32)
        gate = jnp.sum(pick(part(4) + part(5) + part(6)), axis=1, keepdims=True)
        gate_ref[pc * pchunk:(pc + 1) * pchunk, :] = jnp.broadcast_to(gate, (pchunk, LANES))


def _route(aff_t, cap):
    e, n = aff_t.shape
    nb = n // LANES
    assert nb <= 256 and cap % 64 == 0 and cap // 64 <= 256
    pchunk = min(1024, cap)
    aff3 = aff_t.reshape(e, nb, LANES)
    return pl.pallas_call(
        functools.partial(_route_kernel, cap=cap, pchunk=pchunk),
        grid=(e,),
        in_specs=[_const_spec(aff3.shape), pl.BlockSpec((1, nb, LANES), lambda i: (i, 0, 0))],
        out_specs=(pl.BlockSpec((1, 1, cap), lambda i: (i, 0, 0)),
                   pl.BlockSpec((cap, LANES), lambda i: (i, 0)),
                   pl.BlockSpec((1, nb, LANES), lambda i: (i, 0, 0)),
                   pl.BlockSpec((1, 1, nb), lambda i: (i, 0, 0))),
        out_shape=(jax.ShapeDtypeStruct((e, 1, cap), I32), jax.ShapeDtypeStruct((e * cap, LANES), F32),
                   jax.ShapeDtypeStruct((e, nb, LANES), I32), jax.ShapeDtypeStruct((e, 1, nb), I32)),
        scratch_shapes=[pltpu.VMEM((e, SUBLANES, LANES), F32), pltpu.VMEM((e, SUBLANES, LANES), F32)],
        name="route",
        compiler_params=_params("arbitrary"),
    )(aff3, aff3)


def _ffn_kernel(idx_ref, nxt_ref, nx2_ref, gate_ref, wg_ref, wu_ref, wd_ref, h2_hbm, ye_ref, xbuf, sem):
    s = pl.program_id(0)
    slot = s % FFN_SLOTS
    ahead = (s + FFN_SLOTS - 1) % FFN_SLOTS

    def row_copy(rows_ref, dst_slot, i):
        return pltpu.make_async_copy(h2_hbm.at[rows_ref[0, 0, i]], xbuf.at[dst_slot, i], sem.at[dst_slot])

    def wait_slot(dst_slot):
        pltpu.make_async_copy(h2_hbm.at[pl.ds(0, FFN_ROWS)], xbuf.at[dst_slot], sem.at[dst_slot]).wait()

    @pl.when(s == 0)
    def _():
        def body(i, carry):
            row_copy(idx_ref, 0, i).start()
            row_copy(nxt_ref, 1, i).start()
            return carry
        lax.fori_loop(0, FFN_ROWS, body, 0, unroll=8)

    wait_slot(slot)
    x = jnp.concatenate([xbuf[slot, :, c, :] for c in range(D_MODEL // LANES)], axis=-1).astype(BF16)
    for i in range(FFN_ROWS):
        row_copy(nx2_ref, ahead, i).start(priority=i % 2)
    a = _dot(x, wg_ref[0])
    u = _dot(x, wu_ref[0])
    hmid = (a * jax.nn.sigmoid(a) * u).astype(BF16)
    ye_ref[...] = _dot(hmid, wd_ref[0]) * gate_ref[:, 0:1]

    @pl.when(s + 1 == pl.num_programs(0))
    def _():
        wait_slot((s + 1) % FFN_SLOTS)
        wait_slot(ahead)


def _ffn(idx, gate, wg, wu, wd, h2, cap):
    e = idx.shape[0]
    tr = FFN_ROWS
    per = cap // tr
    nsteps = e * per
    assert cap % tr == 0 and FFN_SLOTS == 3 and nsteps >= FFN_SLOTS
    wspec = lambda shp: pl.BlockSpec((1,) + shp, lambda s: (s // per, 0, 0))
    rows = lambda shift: pl.BlockSpec(
        (1, 1, tr), lambda s: (jnp.minimum(s + shift, nsteps - 1) // per, 0, jnp.minimum(s + shift, nsteps - 1) % per),
        memory_space=pltpu.SMEM)
    return pl.pallas_call(
        _ffn_kernel,
        grid=(nsteps,),
        in_specs=[rows(0), rows(1), rows(2), pl.BlockSpec((tr, LANES), lambda s: (s, 0)),
                  wspec((D_MODEL, D_FF)), wspec((D_MODEL, D_FF)), wspec((D_FF, D_MODEL)),
                  pl.BlockSpec(memory_space=pl.ANY)],
        out_specs=pl.BlockSpec((tr, D_MODEL), lambda s: (s, 0)),
        out_shape=jax.ShapeDtypeStruct((e * cap, D_MODEL), F32),
        scratch_shapes=[pltpu.VMEM((FFN_SLOTS, tr, D_MODEL // LANES, LANES), F32),
                        pltpu.SemaphoreType.DMA((FFN_SLOTS,))],
        name="ffn",
        compiler_params=_params("arbitrary"),
    )(idx, idx, idx, gate, wg, wu, wd, h2)


def _combine_kernel(bst_ref, pos_ref, h2_ref, g_ref, b_ref, ye_hbm, o_ref, ybuf, sem, *, cap):
    e_n = N_EXPERTS
    w = CMB_WIN
    step = w - SUBLANES
    per = CMB_TILE // LANES
    t = pl.program_id(0)
    last = pl.num_programs(0) - 1
    slot = t % 2
    wi = lax.broadcasted_iota(I32, (w, CMB_TILE), 0)

    def windows(tile, r):
        lo = [bst_ref[e, tile * per] + r * step for e in range(e_n)]
        src = [pl.multiple_of(jnp.minimum(e * cap + (lo[e] // SUBLANES) * SUBLANES, e_n * cap - w), SUBLANES)
               for e in range(e_n)]
        return lo, src

    def copies(src, dst_slot):
        return [pltpu.make_async_copy(ye_hbm.at[pl.ds(src[e], w)], ybuf.at[dst_slot, pl.ds(e * w, w)],
                                      sem.at[dst_slot, e]) for e in range(e_n)]

    def contribution(lo, src, buf_slot):
        onehot = []
        for e in range(e_n):
            base = src[e] - e * cap
            p = pos_ref[e:e + 1, :]
            this_round = jnp.logical_and(p >= lo[e], p < lo[e] + step)
            onehot.append(jnp.logical_and(p == base + wi, this_round).astype(BF16))
        oh_t = jnp.concatenate(onehot, axis=0)
        yv = ybuf[buf_slot]
        hi = yv.astype(BF16)
        lo_part = (yv - hi.astype(F32)).astype(BF16)
        return _dot_tn(oh_t, hi) + _dot_tn(oh_t, lo_part)

    lo0, src0 = windows(t, 0)

    @pl.when(t == 0)
    def _():
        for c in copies(src0, 0):
            c.start()

    _, src_next = windows(jnp.minimum(t + 1, last), 0)
    for c in copies(src_next, 1 - slot):
        c.start()
    for c in copies(src0, slot):
        c.wait()
    y = contribution(lo0, src0, slot)

    counts = [bst_ref[e, (t + 1) * per] - bst_ref[e, t * per] for e in range(e_n)]
    nrounds = functools.reduce(jnp.maximum, [(c + (step - 1)) // step for c in counts])

    def extra_round(r, y):
        lo, src = windows(t, r)
        for c in copies(src, slot):
            c.start()
        for c in copies(src, slot):
            c.wait()
        return y + contribution(lo, src, slot)

    y = lax.fori_loop(1, nrounds, extra_round, y)
    o_ref[...] = _layer_norm(DEEPNORM_ALPHA * h2_ref[...] + y, g_ref[...], b_ref[...])

    @pl.when(t == last)
    def _():
        for c in copies(src_next, 1 - slot):
            c.wait()


def _combine(bst, pos, h2, g, b, ye, cap):
    n = h2.shape[0]
    tt = CMB_TILE
    assert cap >= CMB_WIN and n % tt == 0
    grid_spec = pltpu.PrefetchScalarGridSpec(
        num_scalar_prefetch=1,
        grid=(n // tt,),
        in_specs=[pl.BlockSpec((N_EXPERTS, tt), lambda i, *_: (0, i)),
                  pl.BlockSpec((tt, D_MODEL), lambda i, *_: (i, 0)),
                  pl.BlockSpec(g.shape, lambda i, *_: (0, 0)), pl.BlockSpec(b.shape, lambda i, *_: (0, 0)),
                  pl.BlockSpec(memory_space=pl.ANY)],
        out_specs=pl.BlockSpec((tt, D_MODEL), lambda i, *_: (i, 0)),
        scratch_shapes=[pltpu.VMEM((2, N_EXPERTS * CMB_WIN, D_MODEL), F32),
                        pltpu.SemaphoreType.DMA((2, N_EXPERTS))],
    )
    return pl.pallas_call(
        functools.partial(_combine_kernel, cap=cap),
        grid_spec=grid_spec,
        out_shape=jax.ShapeDtypeStruct((n, D_MODEL), F32),
        name="combine_ln3",
        compiler_params=_params("arbitrary"),
    )(bst, pos, h2, g, b, ye)


def _prepare(ln_in_g, ln_in_b, w_in, na_rpb, gla_gate_w2, gla_gate_b, gla_norm_g, w_out, ln1_g, ln1_b,
             mem_wq, mem_wkv, mem_wo, ln2_g, ln2_b, w_router, w_gate, w_up, w_down, ln3_g, ln3_b):
    row = lambda v: v.reshape(1, -1).astype(F32)
    w = w_in[0]
    o = np.cumsum((0, NA_WIDTH, NA_WIDTH, NA_WIDTH, GLA_KEY_WIDTH, GLA_KEY_WIDTH, GLA_WIDTH, GLA_WIDTH,
                   2 * GLA_GATE_RANK))
    wgk = jnp.zeros((D_MODEL, LANES), F32).at[:, :2 * GLA_GATE_RANK].set(w[:, o[7]:o[8]])
    w2 = jnp.zeros((LANES, 2 * GLA_KEY_WIDTH), F32)
    for s in range(2):
        w2 = w2.at[s * GLA_GATE_RANK:(s + 1) * GLA_GATE_RANK,
                   s * GLA_KEY_WIDTH:(s + 1) * GLA_KEY_WIDTH].set(gla_gate_w2[0, s])
    tab, kaug = _na_tables(na_rpb[0])
    wr_t = w_router[0].T.astype(F32)
    wr_hi = wr_t.astype(BF16)
    wr_lo = (wr_t - wr_hi.astype(F32)).astype(BF16)
    return dict(
        ln_in=(row(ln_in_g), row(ln_in_b)),
        wna=w[:, o[0]:o[3]].astype(BF16), wqk=w[:, o[3]:o[5]].astype(BF16), wv=w[:, o[5]:o[6]].astype(BF16),
        wr=w[:, o[6]:o[7]].astype(BF16), wgk=wgk.astype(BF16), w2=w2.astype(BF16),
        gb=gla_gate_b[0].reshape(1, -1).astype(F32),
        tab=tab, kaug=kaug, ng=row(gla_norm_g[0]),
        wo1=w_out[0][:NA_WIDTH].astype(BF16), wo2=w_out[0][NA_WIDTH:].astype(BF16),
        ln1=(row(ln1_g[0]), row(ln1_b[0])),
        wq=mem_wq[0].astype(BF16), wkv=mem_wkv[0].astype(BF16), wmo=mem_wo[0].astype(BF16),
        ln2=(row(ln2_g[0]), row(ln2_b[0])),
        wr_hi=wr_hi, wr_lo=wr_lo,
        wg=w_gate[0].astype(BF16), wu=w_up[0].astype(BF16), wd=w_down[0].astype(BF16),
        ln3=(row(ln3_g[0]), row(ln3_b[0])),
    )


def _trunk(x, mem, p):
    batch, seq, _ = x.shape
    n = batch * seq
    cap = EC_CAPACITY_FACTOR * n // N_EXPERTS
    h0, qkv, qk, v, r, la = _in_proj(x.reshape(n, D_MODEL), *p["ln_in"], p["wna"], p["wqk"], p["wv"], p["wr"],
                                     p["wgk"], p["w2"], p["gb"])
    na = _na(qkv, p["tab"], p["kaug"], batch, seq)
    of, ob = _gla(qk, v, la, batch, seq)
    kv = _mem_kv(mem.reshape(batch * MEM_TOKENS, D_MODEL), p["wkv"])
    h2, h2t, aff_t = _xattn(na, of, ob, r, p["ng"], h0, p["wo1"], p["wo2"], *p["ln1"], kv, p["wq"], p["wmo"],
                            *p["ln2"], p["wr_hi"], p["wr_lo"], batch, seq)
    idx, gate, pos, bst = _route(aff_t, cap)
    ye = _ffn(idx, gate, p["wg"], p["wu"], p["wd"], h2t, cap)
    bst = jnp.concatenate([bst[:, 0, :], jnp.full((N_EXPERTS, 1), cap, I32)], axis=1)
    out = _combine(bst, pos.reshape(N_EXPERTS, n), h2, *p["ln3"], ye, cap)
    return out.reshape(batch, seq, D_MODEL)


def kernel(x_prompt, x_sample, mem_prompt, mem_sample, ln_in_g, ln_in_b, w_in, na_rpb, gla_gate_w2, gla_gate_b,
           gla_norm_g, w_out, ln1_g, ln1_b, mem_wq, mem_wkv, mem_wo, ln2_g, ln2_b, w_router, w_gate, w_up, w_down,
           ln3_g, ln3_b):
    p = _prepare(ln_in_g, ln_in_b, w_in, na_rpb, gla_gate_w2, gla_gate_b, gla_norm_g, w_out, ln1_g, ln1_b,
                 mem_wq, mem_wkv, mem_wo, ln2_g, ln2_b, w_router, w_gate, w_up, w_down, ln3_g, ln3_b)
    return _trunk(x_prompt, mem_prompt, p), _trunk(x_sample, mem_sample, p)
```

```python
import functools

import numpy as np
import jax
import jax.numpy as jnp
from jax import lax
from jax.experimental import pallas as pl
from jax.experimental.pallas import tpu as pltpu

F32 = jnp.float32
BF16 = jnp.bfloat16
I32 = jnp.int32

D_MODEL = 1024
GRID_W = 64
NA_HEADS = 8
NA_HEAD_DIM = 64
NA_WIDTH = NA_HEADS * NA_HEAD_DIM
NA_WIN_ROWS = 8
NA_WIN_COLS = 16
GLA_HEADS = 4
GLA_DK = 64
GLA_DV = 128
GLA_KEY_WIDTH = GLA_HEADS * GLA_DK
GLA_WIDTH = GLA_HEADS * GLA_DV
GLA_GATE_RANK = 16
GLA_GATE_NORM = 16.0
GLA_CHUNK = 64
MEM_TOKENS = 256
MEM_HEADS = 4
MEM_HEAD_DIM = D_MODEL // MEM_HEADS
N_EXPERTS = 16
EC_CAPACITY_FACTOR = 2
D_FF = 2 * D_MODEL
LN_EPS = 1e-5
RMS_EPS = 1e-5
DEPTH = 1
DEEPNORM_ALPHA = (2 * DEPTH) ** 0.25

LANES = 128
SUBLANES = 8
NEG = -1e30
VMEM_LIMIT = 56 * 1024 * 1024

NA_QROWS = 8
NA_SLAB = 16
NA_SUB = 4
NA_AHEAD = 1
GLA_SUPER = 512
GLA_SEQS = 1
TOK_TILE = 512
SUB_TILE = 256
XATTN_SUB = 512
FFN_ROWS = 512
FFN_SLOTS = 3
CMB_TILE = 256
CMB_WIN = 64


def _dot(a, b):
    return jnp.dot(a, b, preferred_element_type=F32)


def _dot_nt(a, b):
    return lax.dot_general(a, b, (((1,), (1,)), ((), ())), preferred_element_type=F32)


def _dot_tn(a, b):
    return lax.dot_general(a, b, (((0,), (0,)), ((), ())), preferred_element_type=F32)


def _layer_norm(x, g, b):
    mu = jnp.mean(x, axis=-1, keepdims=True)
    xc = x - mu
    var = jnp.mean(xc * xc, axis=-1, keepdims=True)
    return xc * lax.rsqrt(var + LN_EPS) * g + b


def _split3(x):
    hi = x.astype(BF16)
    r1 = x - hi.astype(F32)
    mid = r1.astype(BF16)
    lo = (r1 - mid.astype(F32)).astype(BF16)
    return hi, mid, lo


def _params(*sem):
    return pltpu.CompilerParams(dimension_semantics=sem, vmem_limit_bytes=VMEM_LIMIT)


def _const_spec(shape):
    nd = len(shape)
    return pl.BlockSpec(shape, lambda *_: (0,) * nd)


def _in_proj_kernel(x_ref, g_ref, b_ref, wna_ref, wqk_ref, wv_ref, wr_ref, wgk_ref, w2_ref, gb_ref,
                    h0_ref, qkv_ref, qk_ref, v_ref, r_ref, la_ref):
    def norm(st, rows):
        h0 = _layer_norm(x_ref[rows, :], g_ref[...], b_ref[...])
        h0_ref[rows, :] = h0
        st["hb"] = h0.astype(BF16)

    def project(st, rows):
        hb = st["hb"]
        qkv_ref[rows, :] = _dot(hb, wna_ref[...]).astype(BF16)
        qk_ref[rows, :] = _dot(hb, wqk_ref[...])
        v_ref[rows, :] = _dot(hb, wv_ref[...]).astype(BF16)
        r_ref[rows, :] = _dot(hb, wr_ref[...])
        st["gk"] = _dot(hb, wgk_ref[...])

    def decay_gate(st, rows):
        z = _dot(st["gk"].astype(BF16), w2_ref[...]) + gb_ref[...]
        log_sig = jnp.minimum(z, 0.0) - jnp.log1p(jnp.exp(-jnp.abs(z)))
        la_ref[rows, :] = log_sig * (1.0 / GLA_GATE_NORM)

    stages = (norm, project, decay_gate)
    tiles = [(dict(), slice(r0, r0 + SUB_TILE)) for r0 in range(0, TOK_TILE, SUB_TILE)]
    for k in range(len(stages) + len(tiles) - 1):
        for t, (st, rows) in enumerate(tiles):
            if 0 <= k - t < len(stages):
                stages[k - t](st, rows)


def _in_proj(x, g, b, wna, wqk, wv, wr, wgk, w2, gb):
    n = x.shape[0]
    tm = TOK_TILE
    row = lambda w: pl.BlockSpec((tm, w), lambda i: (i, 0))
    outs = (
        jax.ShapeDtypeStruct((n, D_MODEL), F32),
        jax.ShapeDtypeStruct((n, 3 * NA_WIDTH), BF16),
        jax.ShapeDtypeStruct((n, 2 * GLA_KEY_WIDTH), F32),
        jax.ShapeDtypeStruct((n, GLA_WIDTH), BF16),
        jax.ShapeDtypeStruct((n, GLA_WIDTH), F32),
        jax.ShapeDtypeStruct((n, 2 * GLA_KEY_WIDTH), F32),
    )
    return pl.pallas_call(
        _in_proj_kernel,
        grid=(n // tm,),
        in_specs=[row(D_MODEL), _const_spec(g.shape), _const_spec(b.shape), _const_spec(wna.shape),
                  _const_spec(wqk.shape), _const_spec(wv.shape), _const_spec(wr.shape),
                  _const_spec(wgk.shape), _const_spec(w2.shape), _const_spec(gb.shape)],
        out_specs=(row(D_MODEL), row(3 * NA_WIDTH), row(2 * GLA_KEY_WIDTH), row(GLA_WIDTH), row(GLA_WIDTH),
                   row(2 * GLA_KEY_WIDTH)),
        out_shape=outs,
        name="ln_in_proj",
        compiler_params=_params("parallel"),
    )(x, g, b, wna, wqk, wv, wr, wgk, w2, gb)


def _na_tables(rpb):
    kw = NA_WIN_COLS
    cols = np.arange(GRID_W)
    col_start = np.clip(cols - kw // 2, 0, GRID_W - kw)
    j = np.arange(GRID_W)
    col_ok = (j[None, :] >= col_start[:, None]) & (j[None, :] < col_start[:, None] + kw)
    col_off = np.clip(j[None, :] - cols[:, None] + (NA_WIN_COLS - 1), 0, 2 * NA_WIN_COLS - 2)
    nblk = NA_SLAB + NA_QROWS
    ro = np.arange(nblk) - 5
    ro_ok = (ro >= 0) & (ro < 2 * NA_WIN_ROWS - 1)
    ro_c = np.clip(ro, 0, 2 * NA_WIN_ROWS - 2)
    t = rpb[:, ro_c][:, :, col_off]
    ok = jnp.asarray(ro_ok[None, :, None, None] & col_ok[None, None])
    t = jnp.where(ok, t, NEG).astype(F32)
    flat = jnp.transpose(t, (0, 2, 1, 3)).reshape(NA_HEADS, GRID_W, nblk * GRID_W)
    shifted = jnp.concatenate([flat[:, :, GRID_W:], jnp.full((NA_HEADS, GRID_W, GRID_W), NEG, F32)], axis=-1)
    tab = jnp.stack([flat, shifted])

    a = np.arange(NA_QROWS)
    i = np.arange(NA_SLAB)
    kaug = np.zeros((3, 2, NA_SLAB * GRID_W, LANES), np.float32)
    for vi, w in enumerate((np.maximum(a, 4), a, np.minimum(a, 4))):
        valid = (i[None, :] >= w[:, None]) & (i[None, :] < w[:, None] + NA_WIN_ROWS)
        per_key = np.repeat(np.where(valid, 0.0, NEG).T, GRID_W, axis=0)
        kaug[vi, 0, :, NA_HEAD_DIM:NA_HEAD_DIM + NA_QROWS] = per_key
        kaug[vi, 1, :, :NA_QROWS] = per_key
    return tab, jnp.asarray(kaug).astype(BF16)


def _na_kernel(q_ref, kp_ref, kc_ref, kn_ref, vp_ref, vc_ref, vn_ref, tab_ref, kaug_ref, o_ref):
    half = NA_QROWS * GRID_W // 2
    k = jnp.concatenate([kp_ref[half:, :], kc_ref[...], kn_ref[:half, :]], axis=0)
    v = jnp.concatenate([vp_ref[half:, :], vc_ref[...], vn_ref[:half, :]], axis=0)
    nq = NA_SUB * GRID_W
    nk = (NA_SUB + NA_WIN_ROWS) * GRID_W
    qlane = lax.broadcasted_iota(I32, (nq, LANES), 1)
    qrow = lax.broadcasted_iota(I32, (nq, LANES), 0) // GRID_W
    klane = lax.broadcasted_iota(I32, (nk, LANES), 1)
    first = qlane < NA_HEAD_DIM

    def scores(hp, u, sub):
        sl = slice(hp * LANES, (hp + 1) * LANES)
        r0 = u * nq
        h = 2 * hp + sub
        q2 = q_ref[r0:r0 + nq, sl] * (NA_HEAD_DIM ** -0.5)
        aug0 = NA_HEAD_DIM if sub == 0 else 0
        onehot = (qlane == aug0 + u * NA_SUB + qrow).astype(BF16)
        qm = jnp.where(first if sub == 0 else jnp.logical_not(first), q2, onehot)
        in_aug = jnp.logical_and(klane >= aug0, klane < aug0 + NA_QROWS)
        km = jnp.where(in_aug, kaug_ref[0, sub, r0:r0 + nk, :], k[r0:r0 + nk, sl])
        s = _dot_nt(qm, km)
        strips = []
        for al in range(NA_SUB):
            a = u * NA_SUB + al
            par = a % 2
            off = ((8 - a) if par == 0 else (7 - a)) * GRID_W + r0
            strips.append(s[al * GRID_W:(al + 1) * GRID_W, :] + tab_ref[par, h, :, off:off + nk])
        return jnp.concatenate(strips, axis=0)

    def attend(s, hp, u):
        m = jnp.max(s, axis=-1, keepdims=True)
        e = jnp.exp(s - m)
        l = jnp.sum(e, axis=-1, keepdims=True)
        return _dot(e.astype(BF16), v[u * nq:u * nq + nk, hp * LANES:(hp + 1) * LANES]) / l

    units = [(hp, u, sub) for hp in range(NA_HEADS // 2) for u in range(NA_QROWS // NA_SUB) for sub in range(2)]
    pending = [scores(*units[j]) for j in range(NA_AHEAD)]
    pv = {}
    for i, (hp, u, sub) in enumerate(units):
        if i + NA_AHEAD < len(units):
            pending.append(scores(*units[i + NA_AHEAD]))
        pv[sub] = attend(pending.pop(0), hp, u)
        if sub == 1:
            o_ref[u * nq:(u + 1) * nq, hp * LANES:(hp + 1) * LANES] = jnp.where(first, pv[0], pv[1]).astype(BF16)


def _na(qkv, tab, kaug, batch, seq):
    n = batch * seq
    nq = NA_QROWS * GRID_W
    nblk = seq // nq
    assert seq % nq == 0 and nblk >= 2
    cur = lambda c: pl.BlockSpec((nq, NA_WIDTH), lambda b, r: (b * nblk + r, c))
    prv = lambda c: pl.BlockSpec((nq, NA_WIDTH), lambda b, r: (b * nblk + jnp.maximum(r - 1, 0), c))
    nxt = lambda c: pl.BlockSpec((nq, NA_WIDTH), lambda b, r: (b * nblk + jnp.minimum(r + 1, nblk - 1), c))
    variant = lambda b, r: (jnp.where(r == 0, 0, jnp.where(r == nblk - 1, 2, 1)), 0, 0, 0)
    return pl.pallas_call(
        _na_kernel,
        grid=(batch, nblk),
        in_specs=[cur(0), prv(1), cur(1), nxt(1), prv(2), cur(2), nxt(2),
                  _const_spec(tab.shape), pl.BlockSpec((1,) + kaug.shape[1:], variant)],
        out_specs=pl.BlockSpec((nq, NA_WIDTH), lambda b, r: (b * nblk + r, 0)),
        out_shape=jax.ShapeDtypeStruct((n, NA_WIDTH), BF16),
        name="nbr_attn",
        compiler_params=_params("parallel", "parallel"),
    )(qkv, qkv, qkv, qkv, qkv, qkv, qkv, tab, kaug)


def _gla_kernel(qkf_ref, vf_ref, laf_ref, qkb_ref, vb_ref, lab_ref, of_ref, ob_ref, sf_ref, sb_ref):
    c = GLA_CHUNK
    kwid = GLA_KEY_WIDTH
    nchunk = GLA_SUPER // c

    @pl.when(pl.program_id(1) == 0)
    def _():
        sf_ref[...] = jnp.zeros_like(sf_ref)
        sb_ref[...] = jnp.zeros_like(sb_ref)

    ri = lax.broadcasted_iota(I32, (GLA_SUPER, GLA_SUPER), 0)
    ci = lax.broadcasted_iota(I32, (GLA_SUPER, GLA_SUPER), 1)
    same_chunk = (ri // c) == (ci // c)
    tri_f = jnp.logical_and(same_chunk, ci <= ri).astype(BF16)
    tri_b = jnp.logical_and(same_chunk, ci >= ri).astype(BF16)
    ar = lax.broadcasted_iota(I32, (c, kwid), 0)
    ac = lax.broadcasted_iota(I32, (c, kwid), 1) % c
    keep_f = ac <= ar
    keep_b = ac >= ar
    wr_ = lax.broadcasted_iota(I32, (kwid, kwid), 0) // c
    wc_ = lax.broadcasted_iota(I32, (kwid, kwid), 1) // GLA_DK
    wmask = wr_ == wc_
    vr_ = lax.broadcasted_iota(I32, (kwid, GLA_WIDTH), 0) // c
    vc_ = lax.broadcasted_iota(I32, (kwid, GLA_WIDTH), 1) // GLA_DV
    vmask = vr_ == vc_
    sr_ = lax.broadcasted_iota(I32, (GLA_WIDTH, kwid), 0) // GLA_DV
    sc_ = lax.broadcasted_iota(I32, (GLA_WIDTH, kwid), 1) // GLA_DK
    smask = sr_ == sc_

    class Stream:
        def __init__(self, qk_ref, v_ref, la_ref, o_ref, st_ref, tri, keep, last, order):
            self.refs = (qk_ref, v_ref, o_ref, st_ref)
            self.keep, self.last, self.order = keep, last, list(order)
            hi, mid, lo = _split3(la_ref[...])
            self.b_all = _dot(tri, hi) + _dot(tri, mid) + _dot(tri, lo)
            self.qe, self.intra, self.upd, self.dec = {}, {}, {}, {}

        def prepare(self, ic):
            qk_ref, v_ref, _, _ = self.refs
            r = slice(ic * c, (ic + 1) * c)
            b = self.b_all[r]
            bl = b[self.last:self.last + 1, :]
            q = qk_ref[r, :kwid] * (GLA_DK ** -0.5)
            k = qk_ref[r, kwid:]
            v = v_ref[r, :]
            qe = (q * jnp.exp(b)).astype(BF16)
            ke = (k * jnp.exp(-b)).astype(BF16)
            kd = (k * jnp.exp(bl - b)).astype(BF16)
            wt = jnp.where(wmask, jnp.concatenate([ke] * GLA_HEADS, axis=0), jnp.zeros((kwid, kwid), BF16))
            a = jnp.where(self.keep, _dot_nt(qe, wt), 0.0)
            vbd = jnp.where(vmask, jnp.concatenate([v] * GLA_HEADS, axis=0),
                            jnp.zeros((kwid, GLA_WIDTH), BF16))
            self.qe[ic] = qe
            self.intra[ic] = _dot(a.astype(BF16), vbd)
            self.upd[ic] = jnp.where(smask, _dot_tn(v, kd), 0.0)
            self.dec[ic] = jnp.exp(bl)

        def advance(self, pos):
            _, _, o_ref, st_ref = self.refs
            ic = self.order[pos]
            st = st_ref[...] if pos == 0 else self.st
            o_ref[ic * c:(ic + 1) * c, :] = self.intra[ic] + _dot_nt(self.qe[ic], st.astype(BF16))
            self.st = st * self.dec[ic] + self.upd[ic]
            if pos == nchunk - 1:
                st_ref[...] = self.st

    streams = []
    for j in range(GLA_SEQS):
        streams.append(Stream(qkf_ref.at[j], vf_ref.at[j], laf_ref.at[j], of_ref.at[j], sf_ref.at[j],
                              tri_f, keep_f, c - 1, range(nchunk)))
        streams.append(Stream(qkb_ref.at[j], vb_ref.at[j], lab_ref.at[j], ob_ref.at[j], sb_ref.at[j],
                              tri_b, keep_b, 0, range(nchunk - 1, -1, -1)))
    for pos in range(nchunk):
        for s in streams:
            s.prepare(s.order[pos])
    for pos in range(nchunk):
        for s in streams:
            s.advance(pos)


def _gla(qk, v, la, batch, seq):
    n = batch * seq
    ns = seq // GLA_SUPER
    g = GLA_SEQS
    assert seq % GLA_SUPER == 0 and batch % g == 0
    fwd = lambda w, c: pl.BlockSpec((g, GLA_SUPER, w), lambda b, s: (b, s, c))
    bwd = lambda w, c: pl.BlockSpec((g, GLA_SUPER, w), lambda b, s: (b, ns - 1 - s, c))
    kw2 = 2 * GLA_KEY_WIDTH
    seqs = lambda t: t.reshape(batch, seq, t.shape[-1])
    state = pltpu.VMEM((g, GLA_WIDTH, GLA_KEY_WIDTH), F32)
    of, ob = pl.pallas_call(
        _gla_kernel,
        grid=(batch // g, ns),
        in_specs=[fwd(kw2, 0), fwd(GLA_WIDTH, 0), fwd(GLA_KEY_WIDTH, 0),
                  bwd(kw2, 0), bwd(GLA_WIDTH, 0), bwd(GLA_KEY_WIDTH, 1)],
        out_specs=(fwd(GLA_WIDTH, 0), bwd(GLA_WIDTH, 0)),
        out_shape=(jax.ShapeDtypeStruct((batch, seq, GLA_WIDTH), F32),
                   jax.ShapeDtypeStruct((batch, seq, GLA_WIDTH), F32)),
        scratch_shapes=[state, state],
        name="gla",
        compiler_params=_params("parallel", "arbitrary"),
    )(seqs(qk), seqs(v), seqs(la), seqs(qk), seqs(v), seqs(la))
    return of.reshape(n, GLA_WIDTH), ob.reshape(n, GLA_WIDTH)


def _mem_kv_kernel(m_ref, w_ref, o_ref):
    o_ref[...] = _dot(m_ref[...].astype(BF16), w_ref[...]).astype(BF16)


def _mem_kv(mem, wkv):
    rows = mem.shape[0]
    tn = wkv.shape[1]
    return pl.pallas_call(
        _mem_kv_kernel,
        grid=(rows // MEM_TOKENS, wkv.shape[1] // tn),
        in_specs=[pl.BlockSpec((MEM_TOKENS, D_MODEL), lambda i, j: (i, 0)),
                  pl.BlockSpec((D_MODEL, tn), lambda i, j: (0, j))],
        out_specs=pl.BlockSpec((MEM_TOKENS, tn), lambda i, j: (i, j)),
        out_shape=jax.ShapeDtypeStruct((rows, wkv.shape[1]), BF16),
        name="mem_kv",
        compiler_params=_params("parallel", "parallel"),
    )(mem, wkv)


def _xattn_kernel(na_ref, of_ref, ob_ref, r_ref, ng_ref, h0_ref, w1_ref, w2_ref, g1_ref, b1_ref,
                  kv_ref, wq_ref, wo_ref, g_ref, b_ref, wrh_ref, wrl_ref, h2_ref, h2t_ref, aff_ref):
    def mix_out(st, rows):
        o = of_ref[rows, :] + ob_ref[rows, :]
        parts = []
        for h in range(GLA_HEADS):
            oh = o[:, h * GLA_DV:(h + 1) * GLA_DV]
            ms = jnp.mean(oh * oh, axis=-1, keepdims=True)
            parts.append(oh * lax.rsqrt(ms + RMS_EPS) * ng_ref[...])
        r = r_ref[rows, :]
        gl = jnp.concatenate(parts, axis=-1) * (r * jax.nn.sigmoid(r))
        mixed = _dot(na_ref[rows, :], w1_ref[...]) + _dot(gl.astype(BF16), w2_ref[...])
        st["h1"] = _layer_norm(DEEPNORM_ALPHA * h0_ref[rows, :] + mixed, g1_ref[...], b1_ref[...])

    def project_q(st, rows):
        st["q"] = _dot(st["h1"].astype(BF16), wq_ref[...]).astype(BF16)

    def attend(st, rows):
        q = st["q"]

        def scores(h):
            sl = slice(h * MEM_HEAD_DIM, (h + 1) * MEM_HEAD_DIM)
            return _dot_nt(q[:, sl], kv_ref[:, sl]) * (MEM_HEAD_DIM ** -0.5)

        outs = []
        s_nxt = scores(0)
        for h in range(MEM_HEADS):
            s = s_nxt
            if h + 1 < MEM_HEADS:
                s_nxt = scores(h + 1)
            m = jnp.max(s, axis=-1, keepdims=True)
            e = jnp.exp(s - m)
            p = (e / jnp.sum(e, axis=-1, keepdims=True)).astype(BF16)
            outs.append(_dot(p, kv_ref[:, D_MODEL + h * MEM_HEAD_DIM:D_MODEL + (h + 1) * MEM_HEAD_DIM]))
        st["o"] = jnp.concatenate(outs, axis=-1).astype(BF16)

    def project_out(st, rows):
        st["y"] = _dot(st["o"], wo_ref[...])

    def norm(st, rows):
        st["h2"] = _layer_norm(DEEPNORM_ALPHA * st["h1"] + st["y"], g_ref[...], b_ref[...])

    def route(st, rows):
        h2 = st["h2"]
        hh = h2.astype(BF16)
        hl = (h2 - hh.astype(F32)).astype(BF16)
        logits = _dot_nt(wrh_ref[...], hh) + _dot_nt(wrh_ref[...], hl) + _dot_nt(wrl_ref[...], hh)
        h2_ref[rows, :] = h2
        for c in range(D_MODEL // LANES):
            h2t_ref[rows, c, :] = h2[:, c * LANES:(c + 1) * LANES]
        m = jnp.max(logits, axis=0, keepdims=True)
        e = jnp.exp(logits - m)
        aff_ref[:, rows] = e / jnp.sum(e, axis=0, keepdims=True)

    stages = (mix_out, project_q, attend, project_out, norm, route)
    tiles = [(dict(), slice(r0, r0 + XATTN_SUB)) for r0 in range(0, TOK_TILE, XATTN_SUB)]
    for k in range(len(stages) + len(tiles) - 1):
        for t, (st, rows) in enumerate(tiles):
            if 0 <= k - t < len(stages):
                stages[k - t](st, rows)


def _xattn(na, of, ob, r, ng, h0, w1, w2, g1, b1, kv, wq, wo, g, b, wrh, wrl, batch, seq):
    n = batch * seq
    tm = TOK_TILE
    nt = seq // tm
    rows = lambda w: pl.BlockSpec((tm, w), lambda i: (i, 0))
    row = rows(D_MODEL)
    consts = lambda *ts: [_const_spec(t.shape) for t in ts]
    return pl.pallas_call(
        _xattn_kernel,
        grid=(n // tm,),
        in_specs=[rows(NA_WIDTH), rows(GLA_WIDTH), rows(GLA_WIDTH), rows(GLA_WIDTH), *consts(ng), row,
                  *consts(w1, w2, g1, b1),
                  pl.BlockSpec((MEM_TOKENS, 2 * D_MODEL), lambda i: (i // nt, 0)),
                  *consts(wq, wo, g, b, wrh, wrl)],
        out_specs=(row, pl.BlockSpec((tm, D_MODEL // LANES, LANES), lambda i: (i, 0, 0)),
                   pl.BlockSpec((N_EXPERTS, tm), lambda i: (0, i))),
        out_shape=(jax.ShapeDtypeStruct((n, D_MODEL), F32),
                   jax.ShapeDtypeStruct((n, D_MODEL // LANES, LANES), F32),
                   jax.ShapeDtypeStruct((N_EXPERTS, n), F32)),
        name="mix_xattn_router",
        compiler_params=_params("parallel"),
    )(na, of, ob, r, ng, h0, w1, w2, g1, b1, kv, wq, wo, g, b, wrh, wrl)


def _route_kernel(all_ref, aff_ref, idx_ref, gate_ref, pos_ref, bst_ref, thr_ref, dthr_ref, *, cap, pchunk):
    e_n = all_ref.shape[0]

    @pl.when(pl.program_id(0) == 0)
    def _():
        a3 = all_ref[...]

        def counts(x):
            return jnp.sum(jnp.sum(x, axis=2, keepdims=True), axis=1, keepdims=True)

        def largest(n_iter, hi0, admits):
            def step(_, carry):
                lo, hi = carry
                mid = lo + ((hi - lo + 1) >> 1)
                ok = admits(mid)
                return jnp.where(ok, mid, lo), jnp.where(ok, hi, mid - 1)
            init = (jnp.zeros((e_n, 1, 1), I32), jnp.full((e_n, 1, 1), hi0, I32))
            return lax.fori_loop(0, n_iter, step, init)[0]

        tbits = largest(31, 0x7F800000,
                        lambda m: counts((a3 >= lax.bitcast_convert_type(m, F32)).astype(F32)) >= cap)
        thr3 = lax.bitcast_convert_type(tbits, F32)
        d3 = a3 - thr3
        fine = lax.bitcast_convert_type(jnp.maximum((tbits >> 23) - 47, 1) << 23, F32)
        kfine = largest(24, (1 << 24) - 1, lambda m: counts((d3 >= m.astype(F32) * fine).astype(F32)) >= cap)
        thr_ref[...] = jnp.broadcast_to(thr3, thr_ref.shape)
        dthr_ref[...] = jnp.broadcast_to(kfine.astype(F32) * fine, dthr_ref.shape)

    a = aff_ref[0]
    nb = a.shape[0]
    ex = pl.program_id(0)
    d = a - thr_ref[ex][0:1, 0:1]
    dthr = dthr_ref[ex][0:1, 0:1]

    def total(x):
        return jnp.sum(jnp.sum(x, axis=0, keepdims=True), axis=1, keepdims=True)

    li = lax.broadcasted_iota(I32, (LANES, LANES), 0)
    lj = lax.broadcasted_iota(I32, (LANES, LANES), 1)
    upper = (li <= lj).astype(BF16)
    bi = lax.broadcasted_iota(I32, (nb, nb), 0)
    bj = lax.broadcasted_iota(I32, (nb, nb), 1)
    strict = (bj < bi).astype(BF16)
    incl = (bi <= bj).astype(BF16)
    ones8 = jnp.ones((8, LANES), BF16)

    def prefix(mask_b):
        rowcum = _dot(mask_b, upper)
        rowtot = jnp.broadcast_to(rowcum[:, LANES - 1:LANES], (nb, LANES)).astype(BF16)
        return rowcum + _dot(strict, rowtot)

    gt = d > dthr
    eq = d == dthr
    need = cap - total(gt.astype(F32))
    sel = jnp.logical_or(gt, jnp.logical_and(eq, prefix(eq.astype(BF16)) <= need))
    selb = sel.astype(BF16)
    cnt = prefix(selb)

    rt_row = _dot_nt(ones8, selb)
    bend = _dot(rt_row.astype(BF16), incl)
    bstart = bend - rt_row
    pos_ref[0] = jnp.where(sel, cnt - 1.0, -1.0).astype(I32)
    bst_ref[0] = bstart[0:1, :].astype(I32)

    cnt_hi = jnp.floor(cnt * (1.0 / 64.0))
    cnt_lo = cnt - 64.0 * cnt_hi
    a_hi, a_mid, a_lo = _split3(a)
    rowid = lax.broadcasted_iota(I32, (nb, LANES), 0).astype(BF16)
    rhs = jnp.concatenate([cnt_hi.astype(BF16), cnt_lo.astype(BF16), selb, rowid, a_hi, a_mid, a_lo], axis=1)
    lanef = lax.broadcasted_iota(I32, (pchunk, LANES), 1).astype(F32)

    for pc in range(cap // pchunk):
        p = (lax.broadcasted_iota(I32, (pchunk, nb), 0) + pc * pchunk).astype(F32)
        g = jnp.logical_and(bstart[0:1, :] <= p, p < bend[0:1, :]).astype(BF16)
        rows = _dot(g, rhs)
        part = lambda k: rows[:, k * LANES:(k + 1) * LANES]
        crow = part(0) * 64.0 + part(1)
        p1 = (lax.broadcasted_iota(I32, (pchunk, LANES), 0) + (pc * pchunk + 1)).astype(F32)
        oh = jnp.logical_and(crow == p1, part(2) > 0.5)
        pick = lambda x: jnp.where(oh, x, 0.0)
        tok_row = _dot_nt(ones8, pick(part(3)).astype(BF16))
        tok_lane = _dot_nt(ones8, pick(lanef).astype(BF16))
        idx_ref[0, :, pc * pchunk:(pc + 1) * pchunk] = (tok_row[0:1] * float(LANES) + tok_lane[0:1]).astype(I32)
        gate = jnp.sum(pick(part(4) + part(5) + part(6)), axis=1, keepdims=True)
        gate_ref[pc * pchunk:(pc + 1) * pchunk, :] = jnp.broadcast_to(gate, (pchunk, LANES))


def _route(aff_t, cap):
    e, n = aff_t.shape
    nb = n // LANES
    assert nb <= 256 and cap % 64 == 0 and cap // 64 <= 256
    pchunk = min(1024, cap)
    aff3 = aff_t.reshape(e, nb, LANES)
    return pl.pallas_call(
        functools.partial(_route_kernel, cap=cap, pchunk=pchunk),
        grid=(e,),
        in_specs=[_const_spec(aff3.shape), pl.BlockSpec((1, nb, LANES), lambda i: (i, 0, 0))],
        out_specs=(pl.BlockSpec((1, 1, cap), lambda i: (i, 0, 0)),
                   pl.BlockSpec((cap, LANES), lambda i: (i, 0)),
                   pl.BlockSpec((1, nb, LANES), lambda i: (i, 0, 0)),
                   pl.BlockSpec((1, 1, nb), lambda i: (i, 0, 0))),
        out_shape=(jax.ShapeDtypeStruct((e, 1, cap), I32), jax.ShapeDtypeStruct((e * cap, LANES), F32),
                   jax.ShapeDtypeStruct((e, nb, LANES), I32), jax.ShapeDtypeStruct((e, 1, nb), I32)),
        scratch_shapes=[pltpu.VMEM((e, SUBLANES, LANES), F32), pltpu.VMEM((e, SUBLANES, LANES), F32)],
        name="route",
        compiler_params=_params("arbitrary"),
    )(aff3, aff3)


def _ffn_kernel(idx_ref, nxt_ref, nx2_ref, gate_ref, wg_ref, wu_ref, wd_ref, h2_hbm, ye_ref, xbuf, sem):
    s = pl.program_id(0)
    slot = s % FFN_SLOTS
    ahead = (s + FFN_SLOTS - 1) % FFN_SLOTS

    def row_copy(rows_ref, dst_slot, i):
        return pltpu.make_async_copy(h2_hbm.at[rows_ref[0, 0, i]], xbuf.at[dst_slot, i], sem.at[dst_slot])

    def wait_slot(dst_slot):
        pltpu.make_async_copy(h2_hbm.at[pl.ds(0, FFN_ROWS)], xbuf.at[dst_slot], sem.at[dst_slot]).wait()

    @pl.when(s == 0)
    def _():
        def body(i, carry):
            row_copy(idx_ref, 0, i).start()
            row_copy(nxt_ref, 1, i).start()
            return carry
        lax.fori_loop(0, FFN_ROWS, body, 0, unroll=8)

    wait_slot(slot)
    x = jnp.concatenate([xbuf[slot, :, c, :] for c in range(D_MODEL // LANES)], axis=-1).astype(BF16)
    for i in range(FFN_ROWS):
        row_copy(nx2_ref, ahead, i).start(priority=i % 2)
    a = _dot(x, wg_ref[0])
    u = _dot(x, wu_ref[0])
    hmid = (a * jax.nn.sigmoid(a) * u).astype(BF16)
    ye_ref[...] = _dot(hmid, wd_ref[0]) * gate_ref[:, 0:1]

    @pl.when(s + 1 == pl.num_programs(0))
    def _():
        wait_slot((s + 1) % FFN_SLOTS)
        wait_slot(ahead)


def _ffn(idx, gate, wg, wu, wd, h2, cap):
    e = idx.shape[0]
    tr = FFN_ROWS
    per = cap // tr
    nsteps = e * per
    assert cap % tr == 0 and FFN_SLOTS == 3 and nsteps >= FFN_SLOTS
    wspec = lambda shp: pl.BlockSpec((1,) + shp, lambda s: (s // per, 0, 0))
    rows = lambda shift: pl.BlockSpec(
        (1, 1, tr), lambda s: (jnp.minimum(s + shift, nsteps - 1) // per, 0, jnp.minimum(s + shift, nsteps - 1) % per),
        memory_space=pltpu.SMEM)
    return pl.pallas_call(
        _ffn_kernel,
        grid=(nsteps,),
        in_specs=[rows(0), rows(1), rows(2), pl.BlockSpec((tr, LANES), lambda s: (s, 0)),
                  wspec((D_MODEL, D_FF)), wspec((D_MODEL, D_FF)), wspec((D_FF, D_MODEL)),
                  pl.BlockSpec(memory_space=pl.ANY)],
        out_specs=pl.BlockSpec((tr, D_MODEL), lambda s: (s, 0)),
        out_shape=jax.ShapeDtypeStruct((e * cap, D_MODEL), F32),
        scratch_shapes=[pltpu.VMEM((FFN_SLOTS, tr, D_MODEL // LANES, LANES), F32),
                        pltpu.SemaphoreType.DMA((FFN_SLOTS,))],
        name="ffn",
        compiler_params=_params("arbitrary"),
    )(idx, idx, idx, gate, wg, wu, wd, h2)


def _combine_kernel(bst_ref, pos_ref, h2_ref, g_ref, b_ref, ye_hbm, o_ref, ybuf, sem, *, cap):
    e_n = N_EXPERTS
    w = CMB_WIN
    step = w - SUBLANES
    per = CMB_TILE // LANES
    t = pl.program_id(0)
    last = pl.num_programs(0) - 1
    slot = t % 2
    wi = lax.broadcasted_iota(I32, (w, CMB_TILE), 0)

    def windows(tile, r):
        lo = [bst_ref[e, tile * per] + r * step for e in range(e_n)]
        src = [pl.multiple_of(jnp.minimum(e * cap + (lo[e] // SUBLANES) * SUBLANES, e_n * cap - w), SUBLANES)
               for e in range(e_n)]
        return lo, src

    def copies(src, dst_slot):
        return [pltpu.make_async_copy(ye_hbm.at[pl.ds(src[e], w)], ybuf.at[dst_slot, pl.ds(e * w, w)],
                                      sem.at[dst_slot, e]) for e in range(e_n)]

    def contribution(lo, src, buf_slot):
        onehot = []
        for e in range(e_n):
            base = src[e] - e * cap
            p = pos_ref[e:e + 1, :]
            this_round = jnp.logical_and(p >= lo[e], p < lo[e] + step)
            onehot.append(jnp.logical_and(p == base + wi, this_round).astype(BF16))
        oh_t = jnp.concatenate(onehot, axis=0)
        yv = ybuf[buf_slot]
        hi = yv.astype(BF16)
        lo_part = (yv - hi.astype(F32)).astype(BF16)
        return _dot_tn(oh_t, hi) + _dot_tn(oh_t, lo_part)

    lo0, src0 = windows(t, 0)

    @pl.when(t == 0)
    def _():
        for c in copies(src0, 0):
            c.start()

    _, src_next = windows(jnp.minimum(t + 1, last), 0)
    for c in copies(src_next, 1 - slot):
        c.start()
    for c in copies(src0, slot):
        c.wait()
    y = contribution(lo0, src0, slot)

    counts = [bst_ref[e, (t + 1) * per] - bst_ref[e, t * per] for e in range(e_n)]
    nrounds = functools.reduce(jnp.maximum, [(c + (step - 1)) // step for c in counts])

    def extra_round(r, y):
        lo, src = windows(t, r)
        for c in copies(src, slot):
            c.start()
        for c in copies(src, slot):
            c.wait()
        return y + contribution(lo, src, slot)

    y = lax.fori_loop(1, nrounds, extra_round, y)
    o_ref[...] = _layer_norm(DEEPNORM_ALPHA * h2_ref[...] + y, g_ref[...], b_ref[...])

    @pl.when(t == last)
    def _():
        for c in copies(src_next, 1 - slot):
            c.wait()


def _combine(bst, pos, h2, g, b, ye, cap):
    n = h2.shape[0]
    tt = CMB_TILE
    assert cap >= CMB_WIN and n % tt == 0
    grid_spec = pltpu.PrefetchScalarGridSpec(
        num_scalar_prefetch=1,
        grid=(n // tt,),
        in_specs=[pl.BlockSpec((N_EXPERTS, tt), lambda i, *_: (0, i)),
                  pl.BlockSpec((tt, D_MODEL), lambda i, *_: (i, 0)),
                  pl.BlockSpec(g.shape, lambda i, *_: (0, 0)), pl.BlockSpec(b.shape, lambda i, *_: (0, 0)),
                  pl.BlockSpec(memory_space=pl.ANY)],
        out_specs=pl.BlockSpec((tt, D_MODEL), lambda i, *_: (i, 0)),
        scratch_shapes=[pltpu.VMEM((2, N_EXPERTS * CMB_WIN, D_MODEL), F32),
                        pltpu.SemaphoreType.DMA((2, N_EXPERTS))],
    )
    return pl.pallas_call(
        functools.partial(_combine_kernel, cap=cap),
        grid_spec=grid_spec,
        out_shape=jax.ShapeDtypeStruct((n, D_MODEL), F32),
        name="combine_ln3",
        compiler_params=_params("arbitrary"),
    )(bst, pos, h2, g, b, ye)


def _prepare(ln_in_g, ln_in_b, w_in, na_rpb, gla_gate_w2, gla_gate_b, gla_norm_g, w_out, ln1_g, ln1_b,
             mem_wq, mem_wkv, mem_wo, ln2_g, ln2_b, w_router, w_gate, w_up, w_down, ln3_g, ln3_b):
    row = lambda v: v.reshape(1, -1).astype(F32)
    w = w_in[0]
    o = np.cumsum((0, NA_WIDTH, NA_WIDTH, NA_WIDTH, GLA_KEY_WIDTH, GLA_KEY_WIDTH, GLA_WIDTH, GLA_WIDTH,
                   2 * GLA_GATE_RANK))
    wgk = jnp.zeros((D_MODEL, LANES), F32).at[:, :2 * GLA_GATE_RANK].set(w[:, o[7]:o[8]])
    w2 = jnp.zeros((LANES, 2 * GLA_KEY_WIDTH), F32)
    for s in range(2):
        w2 = w2.at[s * GLA_GATE_RANK:(s + 1) * GLA_GATE_RANK,
                   s * GLA_KEY_WIDTH:(s + 1) * GLA_KEY_WIDTH].set(gla_gate_w2[0, s])
    tab, kaug = _na_tables(na_rpb[0])
    wr_t = w_router[0].T.astype(F32)
    wr_hi = wr_t.astype(BF16)
    wr_lo = (wr_t - wr_hi.astype(F32)).astype(BF16)
    return dict(
        ln_in=(row(ln_in_g), row(ln_in_b)),
        wna=w[:, o[0]:o[3]].astype(BF16), wqk=w[:, o[3]:o[5]].astype(BF16), wv=w[:, o[5]:o[6]].astype(BF16),
        wr=w[:, o[6]:o[7]].astype(BF16), wgk=wgk.astype(BF16), w2=w2.astype(BF16),
        gb=gla_gate_b[0].reshape(1, -1).astype(F32),
        tab=tab, kaug=kaug, ng=row(gla_norm_g[0]),
        wo1=w_out[0][:NA_WIDTH].astype(BF16), wo2=w_out[0][NA_WIDTH:].astype(BF16),
        ln1=(row(ln1_g[0]), row(ln1_b[0])),
        wq=mem_wq[0].astype(BF16), wkv=mem_wkv[0].astype(BF16), wmo=mem_wo[0].astype(BF16),
        ln2=(row(ln2_g[0]), row(ln2_b[0])),
        wr_hi=wr_hi, wr_lo=wr_lo,
        wg=w_gate[0].astype(BF16), wu=w_up[0].astype(BF16), wd=w_down[0].astype(BF16),
        ln3=(row(ln3_g[0]), row(ln3_b[0])),
    )


def _trunk(x, mem, p):
    batch, seq, _ = x.shape
    n = batch * seq
    cap = EC_CAPACITY_FACTOR * n // N_EXPERTS
    h0, qkv, qk, v, r, la = _in_proj(x.reshape(n, D_MODEL), *p["ln_in"], p["wna"], p["wqk"], p["wv"], p["wr"],
                                     p["wgk"], p["w2"], p["gb"])
    na = _na(qkv, p["tab"], p["kaug"], batch, seq)
    of, ob = _gla(qk, v, la, batch, seq)
    kv = _mem_kv(mem.reshape(batch * MEM_TOKENS, D_MODEL), p["wkv"])
    h2, h2t, aff_t = _xattn(na, of, ob, r, p["ng"], h0, p["wo1"], p["wo2"], *p["ln1"], kv, p["wq"], p["wmo"],
                            *p["ln2"], p["wr_hi"], p["wr_lo"], batch, seq)
    idx, gate, pos, bst = _route(aff_t, cap)
    ye = _ffn(idx, gate, p["wg"], p["wu"], p["wd"], h2t, cap)
    bst = jnp.concatenate([bst[:, 0, :], jnp.full((N_EXPERTS, 1), cap, I32)], axis=1)
    out = _combine(bst, pos.reshape(N_EXPERTS, n), h2, *p["ln3"], ye, cap)
    return out.reshape(batch, seq, D_MODEL)


def kernel(x_prompt, x_sample, mem_prompt, mem_sample, ln_in_g, ln_in_b, w_in, na_rpb, gla_gate_w2, gla_gate_b,
           gla_norm_g, w_out, ln1_g, ln1_b, mem_wq, mem_wkv, mem_wo, ln2_g, ln2_b, w_router, w_gate, w_up, w_down,
           ln3_g, ln3_b):
    p = _prepare(ln_in_g, ln_in_b, w_in, na_rpb, gla_gate_w2, gla_gate_b, gla_norm_g, w_out, ln1_g, ln1_b,
                 mem_wq, mem_wkv, mem_wo, ln2_g, ln2_b, w_router, w_gate, w_up, w_down, ln3_g, ln3_b)
    return _trunk(x_prompt, mem_prompt, p), _trunk(x_sample, mem_sample, p)
```

```python
import functools

import numpy as np
import jax
import jax.numpy as jnp
from jax import lax
from jax.experimental import pallas as pl
from jax.experimental.pallas import tpu as pltpu

F32 = jnp.float32
BF16 = jnp.bfloat16
I32 = jnp.int32

D_MODEL = 1024
GRID_W = 64
NA_HEADS = 8
NA_HEAD_DIM = 64
NA_WIDTH = NA_HEADS * NA_HEAD_DIM
NA_WIN_ROWS = 8
NA_WIN_COLS = 16
GLA_HEADS = 4
GLA_DK = 64
GLA_DV = 128
GLA_KEY_WIDTH = GLA_HEADS * GLA_DK
GLA_WIDTH = GLA_HEADS * GLA_DV
GLA_GATE_RANK = 16
GLA_GATE_NORM = 16.0
GLA_CHUNK = 64
MEM_TOKENS = 256
MEM_HEADS = 4
MEM_HEAD_DIM = D_MODEL // MEM_HEADS
N_EXPERTS = 16
EC_CAPACITY_FACTOR = 2
D_FF = 2 * D_MODEL
LN_EPS = 1e-5
RMS_EPS = 1e-5
DEPTH = 1
DEEPNORM_ALPHA = (2 * DEPTH) ** 0.25

LANES = 128
SUBLANES = 8
NEG = -1e30
VMEM_LIMIT = 56 * 1024 * 1024

NA_QROWS = 8
NA_SLAB = 16
NA_SUB = 4
NA_AHEAD = 1
GLA_SUPER = 512
GLA_SEQS = 1
TOK_TILE = 512
IN_TILE = 1024
SUB_TILE = 256
XATTN_SUB = 512
FFN_ROWS = 512
FFN_SLOTS = 3
CMB_TILE = 256
CMB_WIN = 64


def _dot(a, b):
    return jnp.dot(a, b, preferred_element_type=F32)


def _dot_nt(a, b):
    return lax.dot_general(a, b, (((1,), (1,)), ((), ())), preferred_element_type=F32)


def _dot_tn(a, b):
    return lax.dot_general(a, b, (((0,), (0,)), ((), ())), preferred_element_type=F32)


def _layer_norm(x, g, b):
    mu = jnp.mean(x, axis=-1, keepdims=True)
    xc = x - mu
    var = jnp.mean(xc * xc, axis=-1, keepdims=True)
    return xc * lax.rsqrt(var + LN_EPS) * g + b


def _split3(x):
    hi = x.astype(BF16)
    r1 = x - hi.astype(F32)
    mid = r1.astype(BF16)
    lo = (r1 - mid.astype(F32)).astype(BF16)
    return hi, mid, lo


def _params(*sem):
    return pltpu.CompilerParams(dimension_semantics=sem, vmem_limit_bytes=VMEM_LIMIT)


def _const_spec(shape):
    nd = len(shape)
    return pl.BlockSpec(shape, lambda *_: (0,) * nd)


def _in_proj_kernel(x_ref, g_ref, b_ref, wcat_ref, w2_ref, gb_ref,
                    h0_ref, qkv_ref, qk_ref, v_ref, r_ref, la_ref):
    def norm(st, rows):
        h0 = _layer_norm(x_ref[rows, :], g_ref[...], b_ref[...])
        h0_ref[rows, :] = h0
        st["hb"] = h0.astype(BF16)

    def project(st, rows):
        y = _dot(st["hb"], wcat_ref[...])
        o = np.cumsum((0, 3 * NA_WIDTH, 2 * GLA_KEY_WIDTH, GLA_WIDTH, GLA_WIDTH, LANES))
        qkv_ref[rows, :] = y[:, o[0]:o[1]].astype(BF16)
        qk_ref[rows, :] = y[:, o[1]:o[2]]
        v_ref[rows, :] = y[:, o[2]:o[3]].astype(BF16)
        r_ref[rows, :] = y[:, o[3]:o[4]]
        st["gk"] = y[:, o[4]:o[5]]

    def decay_gate(st, rows):
        z = _dot(st["gk"].astype(BF16), w2_ref[...]) + gb_ref[...]
        log_sig = jnp.minimum(z, 0.0) - jnp.log1p(jnp.exp(-jnp.abs(z)))
        la_ref[rows, :] = log_sig * (1.0 / GLA_GATE_NORM)

    stages = (norm, project, decay_gate)
    tiles = [(dict(), slice(r0, r0 + SUB_TILE)) for r0 in range(0, IN_TILE, SUB_TILE)]
    for k in range(len(stages) + len(tiles) - 1):
        for t, (st, rows) in enumerate(tiles):
            if 0 <= k - t < len(stages):
                stages[k - t](st, rows)


def _in_proj(x, g, b, wna, wqk, wv, wr, wgk, w2, gb):
    n = x.shape[0]
    tm = IN_TILE
    wcat = jnp.concatenate([wna, wqk, wv, wr, wgk], axis=1)
    row = lambda w: pl.BlockSpec((tm, w), lambda i: (i, 0))
    outs = (
        jax.ShapeDtypeStruct((n, D_MODEL), F32),
        jax.ShapeDtypeStruct((n, 3 * NA_WIDTH), BF16),
        jax.ShapeDtypeStruct((n, 2 * GLA_KEY_WIDTH), F32),
        jax.ShapeDtypeStruct((n, GLA_WIDTH), BF16),
        jax.ShapeDtypeStruct((n, GLA_WIDTH), F32),
        jax.ShapeDtypeStruct((n, 2 * GLA_KEY_WIDTH), F32),
    )
    return pl.pallas_call(
        _in_proj_kernel,
        grid=(n // tm,),
        in_specs=[row(D_MODEL), _const_spec(g.shape), _const_spec(b.shape), pl.BlockSpec(wcat.shape, lambda i: (0, 0), pipeline_mode=pl.Buffered(1)),
                  _const_spec(w2.shape), _const_spec(gb.shape)],
        out_specs=(row(D_MODEL), row(3 * NA_WIDTH), row(2 * GLA_KEY_WIDTH), row(GLA_WIDTH), row(GLA_WIDTH),
                   row(2 * GLA_KEY_WIDTH)),
        out_shape=outs,
        name="ln_in_proj",
        compiler_params=_params("parallel"),
    )(x, g, b, wcat, w2, gb)


def _na_tables(rpb):
    kw = NA_WIN_COLS
    cols = np.arange(GRID_W)
    col_start = np.clip(cols - kw // 2, 0, GRID_W - kw)
    j = np.arange(GRID_W)
    col_ok = (j[None, :] >= col_start[:, None]) & (j[None, :] < col_start[:, None] + kw)
    col_off = np.clip(j[None, :] - cols[:, None] + (NA_WIN_COLS - 1), 0, 2 * NA_WIN_COLS - 2)
    nblk = NA_SLAB + NA_QROWS
    ro = np.arange(nblk) - 5
    ro_ok = (ro >= 0) & (ro < 2 * NA_WIN_ROWS - 1)
    ro_c = np.clip(ro, 0, 2 * NA_WIN_ROWS - 2)
    t = rpb[:, ro_c][:, :, col_off]
    ok = jnp.asarray(ro_ok[None, :, None, None] & col_ok[None, None])
    t = jnp.where(ok, t, NEG).astype(F32)
    flat = jnp.transpose(t, (0, 2, 1, 3)).reshape(NA_HEADS, GRID_W, nblk * GRID_W)
    shifted = jnp.concatenate([flat[:, :, GRID_W:], jnp.full((NA_HEADS, GRID_W, GRID_W), NEG, F32)], axis=-1)
    tab = jnp.stack([flat, shifted])

    a = np.arange(NA_QROWS)
    i = np.arange(NA_SLAB)
    kaug = np.zeros((3, 2, NA_SLAB * GRID_W, LANES), np.float32)
    for vi, w in enumerate((np.maximum(a, 4), a, np.minimum(a, 4))):
        valid = (i[None, :] >= w[:, None]) & (i[None, :] < w[:, None] + NA_WIN_ROWS)
        per_key = np.repeat(np.where(valid, 0.0, NEG).T, GRID_W, axis=0)
        kaug[vi, 0, :, NA_HEAD_DIM:NA_HEAD_DIM + NA_QROWS] = per_key
        kaug[vi, 1, :, :NA_QROWS] = per_key
    return tab, jnp.asarray(kaug).astype(BF16)


def _na_kernel(q_ref, kp_ref, kc_ref, kn_ref, vp_ref, vc_ref, vn_ref, tab_ref, kaug_ref, o_ref):
    half = NA_QROWS * GRID_W // 2
    k = jnp.concatenate([kp_ref[half:, :], kc_ref[...], kn_ref[:half, :]], axis=0)
    v = jnp.concatenate([vp_ref[half:, :], vc_ref[...], vn_ref[:half, :]], axis=0)
    nq = NA_SUB * GRID_W
    nk = (NA_SUB + NA_WIN_ROWS) * GRID_W
    qlane = lax.broadcasted_iota(I32, (nq, LANES), 1)
    qrow = lax.broadcasted_iota(I32, (nq, LANES), 0) // GRID_W
    klane = lax.broadcasted_iota(I32, (nk, LANES), 1)
    first = qlane < NA_HEAD_DIM

    def scores(hp, u, sub):
        sl = slice(hp * LANES, (hp + 1) * LANES)
        r0 = u * nq
        h = 2 * hp + sub
        q2 = q_ref[r0:r0 + nq, sl] * (NA_HEAD_DIM ** -0.5)
        aug0 = NA_HEAD_DIM if sub == 0 else 0
        onehot = (qlane == aug0 + u * NA_SUB + qrow).astype(BF16)
        qm = jnp.where(first if sub == 0 else jnp.logical_not(first), q2, onehot)
        in_aug = jnp.logical_and(klane >= aug0, klane < aug0 + NA_QROWS)
        km = jnp.where(in_aug, kaug_ref[0, sub, r0:r0 + nk, :], k[r0:r0 + nk, sl])
        s = _dot_nt(qm, km)
        strips = []
        for al in range(NA_SUB):
            a = u * NA_SUB + al
            par = a % 2
            off = ((8 - a) if par == 0 else (7 - a)) * GRID_W + r0
            strips.append(s[al * GRID_W:(al + 1) * GRID_W, :] + tab_ref[par, h, :, off:off + nk])
        return jnp.concatenate(strips, axis=0)

    def attend(s, hp, u):
        m = jnp.max(s, axis=-1, keepdims=True)
        e = jnp.exp(s - m)
        l = jnp.sum(e, axis=-1, keepdims=True)
        return _dot(e.astype(BF16), v[u * nq:u * nq + nk, hp * LANES:(hp + 1) * LANES]) / l

    units = [(hp, u, sub) for hp in range(NA_HEADS // 2) for u in range(NA_QROWS // NA_SUB) for sub in range(2)]
    pending = [scores(*units[j]) for j in range(NA_AHEAD)]
    pv = {}
    for i, (hp, u, sub) in enumerate(units):
        if i + NA_AHEAD < len(units):
            pending.append(scores(*units[i + NA_AHEAD]))
        pv[sub] = attend(pending.pop(0), hp, u)
        if sub == 1:
            o_ref[u * nq:(u + 1) * nq, hp * LANES:(hp + 1) * LANES] = jnp.where(first, pv[0], pv[1]).astype(BF16)


def _na(qkv, tab, kaug, batch, seq):
    n = batch * seq
    nq = NA_QROWS * GRID_W
    nblk = seq // nq
    assert seq % nq == 0 and nblk >= 2
    cur = lambda c: pl.BlockSpec((nq, NA_WIDTH), lambda b, r: (b * nblk + r, c))
    prv = lambda c: pl.BlockSpec((nq, NA_WIDTH), lambda b, r: (b * nblk + jnp.maximum(r - 1, 0), c))
    nxt = lambda c: pl.BlockSpec((nq, NA_WIDTH), lambda b, r: (b * nblk + jnp.minimum(r + 1, nblk - 1), c))
    variant = lambda b, r: (jnp.where(r == 0, 0, jnp.where(r == nblk - 1, 2, 1)), 0, 0, 0)
    return pl.pallas_call(
        _na_kernel,
        grid=(batch, nblk),
        in_specs=[cur(0), prv(1), cur(1), nxt(1), prv(2), cur(2), nxt(2),
                  _const_spec(tab.shape), pl.BlockSpec((1,) + kaug.shape[1:], variant)],
        out_specs=pl.BlockSpec((nq, NA_WIDTH), lambda b, r: (b * nblk + r, 0)),
        out_shape=jax.ShapeDtypeStruct((n, NA_WIDTH), BF16),
        name="nbr_attn",
        compiler_params=_params("parallel", "parallel"),
    )(qkv, qkv, qkv, qkv, qkv, qkv, qkv, tab, kaug)


def _gla_kernel(qkf_ref, vf_ref, laf_ref, qkb_ref, vb_ref, lab_ref, of_ref, ob_ref, sf_ref, sb_ref):
    c = GLA_CHUNK
    kwid = GLA_KEY_WIDTH
    nchunk = GLA_SUPER // c

    @pl.when(pl.program_id(1) == 0)
    def _():
        sf_ref[...] = jnp.zeros_like(sf_ref)
        sb_ref[...] = jnp.zeros_like(sb_ref)

    ri = lax.broadcasted_iota(I32, (GLA_SUPER, GLA_SUPER), 0)
    ci = lax.broadcasted_iota(I32, (GLA_SUPER, GLA_SUPER), 1)
    same_chunk = (ri // c) == (ci // c)
    tri_f = jnp.logical_and(same_chunk, ci <= ri).astype(BF16)
    tri_b = jnp.logical_and(same_chunk, ci >= ri).astype(BF16)
    ar = lax.broadcasted_iota(I32, (c, kwid), 0)
    ac = lax.broadcasted_iota(I32, (c, kwid), 1) % c
    keep_f = ac <= ar
    keep_b = ac >= ar
    wr_ = lax.broadcasted_iota(I32, (kwid, kwid), 0) // c
    wc_ = lax.broadcasted_iota(I32, (kwid, kwid), 1) // GLA_DK
    wmask = wr_ == wc_
    vr_ = lax.broadcasted_iota(I32, (kwid, GLA_WIDTH), 0) // c
    vc_ = lax.broadcasted_iota(I32, (kwid, GLA_WIDTH), 1) // GLA_DV
    vmask = vr_ == vc_
    sr_ = lax.broadcasted_iota(I32, (GLA_WIDTH, kwid), 0) // GLA_DV
    sc_ = lax.broadcasted_iota(I32, (GLA_WIDTH, kwid), 1) // GLA_DK
    smask = sr_ == sc_

    class Stream:
        def __init__(self, qk_ref, v_ref, la_ref, o_ref, st_ref, tri, keep, last, order):
            self.refs = (qk_ref, v_ref, o_ref, st_ref)
            self.keep, self.last, self.order = keep, last, list(order)
            hi, mid, lo = _split3(la_ref[...])
            self.b_all = _dot(tri, hi) + _dot(tri, mid) + _dot(tri, lo)
            self.qe, self.intra, self.upd, self.dec = {}, {}, {}, {}

        def prepare(self, ic):
            qk_ref, v_ref, _, _ = self.refs
            r = slice(ic * c, (ic + 1) * c)
            b = self.b_all[r]
            bl = b[self.last:self.last + 1, :]
            q = qk_ref[r, :kwid] * (GLA_DK ** -0.5)
            k = qk_ref[r, kwid:]
            v = v_ref[r, :]
            qe = (q * jnp.exp(b)).astype(BF16)
            ke = (k * jnp.exp(-b)).astype(BF16)
            kd = (k * jnp.exp(bl - b)).astype(BF16)
            wt = jnp.where(wmask, jnp.concatenate([ke] * GLA_HEADS, axis=0), jnp.zeros((kwid, kwid), BF16))
            a = jnp.where(self.keep, _dot_nt(qe, wt), 0.0)
            vbd = jnp.where(vmask, jnp.concatenate([v] * GLA_HEADS, axis=0),
                            jnp.zeros((kwid, GLA_WIDTH), BF16))
            self.qe[ic] = qe
            self.intra[ic] = _dot(a.astype(BF16), vbd)
            self.upd[ic] = jnp.where(smask, _dot_tn(v, kd), 0.0)
            self.dec[ic] = jnp.exp(bl)

        def advance(self, pos):
            _, _, o_ref, st_ref = self.refs
            ic = self.order[pos]
            st = st_ref[...] if pos == 0 else self.st
            o_ref[ic * c:(ic + 1) * c, :] = self.intra[ic] + _dot_nt(self.qe[ic], st.astype(BF16))
            self.st = st * self.dec[ic] + self.upd[ic]
            if pos == nchunk - 1:
                st_ref[...] = self.st

    streams = []
    for j in range(GLA_SEQS):
        streams.append(Stream(qkf_ref.at[j], vf_ref.at[j], laf_ref.at[j], of_ref.at[j], sf_ref.at[j],
                              tri_f, keep_f, c - 1, range(nchunk)))
        streams.append(Stream(qkb_ref.at[j], vb_ref.at[j], lab_ref.at[j], ob_ref.at[j], sb_ref.at[j],
                              tri_b, keep_b, 0, range(nchunk - 1, -1, -1)))
    for pos in range(nchunk):
        for s in streams:
            s.prepare(s.order[pos])
    for pos in range(nchunk):
        for s in streams:
            s.advance(pos)


def _gla(qk, v, la, batch, seq):
    n = batch * seq
    ns = seq // GLA_SUPER
    g = GLA_SEQS
    assert seq % GLA_SUPER == 0 and batch % g == 0
    fwd = lambda w, c: pl.BlockSpec((g, GLA_SUPER, w), lambda b, s: (b, s, c))
    bwd = lambda w, c: pl.BlockSpec((g, GLA_SUPER, w), lambda b, s: (b, ns - 1 - s, c))
    kw2 = 2 * GLA_KEY_WIDTH
    seqs = lambda t: t.reshape(batch, seq, t.shape[-1])
    state = pltpu.VMEM((g, GLA_WIDTH, GLA_KEY_WIDTH), F32)
    of, ob = pl.pallas_call(
        _gla_kernel,
        grid=(batch // g, ns),
        in_specs=[fwd(kw2, 0), fwd(GLA_WIDTH, 0), fwd(GLA_KEY_WIDTH, 0),
                  bwd(kw2, 0), bwd(GLA_WIDTH, 0), bwd(GLA_KEY_WIDTH, 1)],
        out_specs=(fwd(GLA_WIDTH, 0), bwd(GLA_WIDTH, 0)),
        out_shape=(jax.ShapeDtypeStruct((batch, seq, GLA_WIDTH), F32),
                   jax.ShapeDtypeStruct((batch, seq, GLA_WIDTH), F32)),
        scratch_shapes=[state, state],
        name="gla",
        compiler_params=_params("parallel", "arbitrary"),
    )(seqs(qk), seqs(v), seqs(la), seqs(qk), seqs(v), seqs(la))
    return of.reshape(n, GLA_WIDTH), ob.reshape(n, GLA_WIDTH)


def _mem_kv_kernel(m_ref, w_ref, o_ref):
    o_ref[...] = _dot(m_ref[...].astype(BF16), w_ref[...]).astype(BF16)


def _mem_kv(mem, wkv):
    rows = mem.shape[0]
    tn = wkv.shape[1]
    return pl.pallas_call(
        _mem_kv_kernel,
        grid=(rows // MEM_TOKENS, wkv.shape[1] // tn),
        in_specs=[pl.BlockSpec((MEM_TOKENS, D_MODEL), lambda i, j: (i, 0)),
                  pl.BlockSpec((D_MODEL, tn), lambda i, j: (0, j))],
        out_specs=pl.BlockSpec((MEM_TOKENS, tn), lambda i, j: (i, j)),
        out_shape=jax.ShapeDtypeStruct((rows, wkv.shape[1]), BF16),
        name="mem_kv",
        compiler_params=_params("parallel", "parallel"),
    )(mem, wkv)


def _xattn_kernel(na_ref, of_ref, ob_ref, r_ref, ng_ref, h0_ref, w1_ref, w2_ref, g1_ref, b1_ref,
                  kv_ref, wq_ref, wo_ref, g_ref, b_ref, wrh_ref, wrl_ref, h2_ref, h2t_ref, aff_ref):
    def mix_out(st, rows):
        o = of_ref[rows, :] + ob_ref[rows, :]
        parts = []
        for h in range(GLA_HEADS):
            oh = o[:, h * GLA_DV:(h + 1) * GLA_DV]
            ms = jnp.mean(oh * oh, axis=-1, keepdims=True)
            parts.append(oh * lax.rsqrt(ms + RMS_EPS) * ng_ref[...])
        r = r_ref[rows, :]
        gl = jnp.concatenate(parts, axis=-1) * (r * jax.nn.sigmoid(r))
        mixed = _dot(na_ref[rows, :], w1_ref[...]) + _dot(gl.astype(BF16), w2_ref[...])
        st["h1"] = _layer_norm(DEEPNORM_ALPHA * h0_ref[rows, :] + mixed, g1_ref[...], b1_ref[...])

    def project_q(st, rows):
        st["q"] = _dot(st["h1"].astype(BF16), wq_ref[...]).astype(BF16)

    def attend(st, rows):
        q = st["q"]

        def scores(h):
            sl = slice(h * MEM_HEAD_DIM, (h + 1) * MEM_HEAD_DIM)
            return _dot_nt(q[:, sl], kv_ref[:, sl]) * (MEM_HEAD_DIM ** -0.5)

        outs = []
        s_nxt = scores(0)
        for h in range(MEM_HEADS):
            s = s_nxt
            if h + 1 < MEM_HEADS:
                s_nxt = scores(h + 1)
            m = jnp.max(s, axis=-1, keepdims=True)
            e = jnp.exp(s - m)
            p = (e / jnp.sum(e, axis=-1, keepdims=True)).astype(BF16)
            outs.append(_dot(p, kv_ref[:, D_MODEL + h * MEM_HEAD_DIM:D_MODEL + (h + 1) * MEM_HEAD_DIM]))
        st["o"] = jnp.concatenate(outs, axis=-1).astype(BF16)

    def project_out(st, rows):
        st["y"] = _dot(st["o"], wo_ref[...])

    def norm(st, rows):
        st["h2"] = _layer_norm(DEEPNORM_ALPHA * st["h1"] + st["y"], g_ref[...], b_ref[...])

    def route(st, rows):
        h2 = st["h2"]
        hh = h2.astype(BF16)
        hl = (h2 - hh.astype(F32)).astype(BF16)
        logits = _dot_nt(wrh_ref[...], hh) + _dot_nt(wrh_ref[...], hl) + _dot_nt(wrl_ref[...], hh)
        h2_ref[rows, :] = h2
        for c in range(D_MODEL // LANES):
            h2t_ref[rows, c, :] = h2[:, c * LANES:(c + 1) * LANES]
        m = jnp.max(logits, axis=0, keepdims=True)
        e = jnp.exp(logits - m)
        aff_ref[:, rows] = e / jnp.sum(e, axis=0, keepdims=True)

    stages = (mix_out, project_q, attend, project_out, norm, route)
    tiles = [(dict(), slice(r0, r0 + XATTN_SUB)) for r0 in range(0, TOK_TILE, XATTN_SUB)]
    for k in range(len(stages) + len(tiles) - 1):
        for t, (st, rows) in enumerate(tiles):
            if 0 <= k - t < len(stages):
                stages[k - t](st, rows)


def _xattn(na, of, ob, r, ng, h0, w1, w2, g1, b1, kv, wq, wo, g, b, wrh, wrl, batch, seq):
    n = batch * seq
    tm = TOK_TILE
    nt = seq // tm
    rows = lambda w: pl.BlockSpec((tm, w), lambda i: (i, 0))
    row = rows(D_MODEL)
    consts = lambda *ts: [_const_spec(t.shape) for t in ts]
    return pl.pallas_call(
        _xattn_kernel,
        grid=(n // tm,),
        in_specs=[rows(NA_WIDTH), rows(GLA_WIDTH), rows(GLA_WIDTH), rows(GLA_WIDTH), *consts(ng), row,
                  *consts(w1, w2, g1, b1),
                  pl.BlockSpec((MEM_TOKENS, 2 * D_MODEL), lambda i: (i // nt, 0)),
                  *consts(wq, wo, g, b, wrh, wrl)],
        out_specs=(row, pl.BlockSpec((tm, D_MODEL // LANES, LANES), lambda i: (i, 0, 0)),
                   pl.BlockSpec((N_EXPERTS, tm), lambda i: (0, i))),
        out_shape=(jax.ShapeDtypeStruct((n, D_MODEL), F32),
                   jax.ShapeDtypeStruct((n, D_MODEL // LANES, LANES), F32),
                   jax.ShapeDtypeStruct((N_EXPERTS, n), F32)),
        name="mix_xattn_router",
        compiler_params=_params("parallel"),
    )(na, of, ob, r, ng, h0, w1, w2, g1, b1, kv, wq, wo, g, b, wrh, wrl)


def _route_kernel(all_ref, aff_ref, idx_ref, gate_ref, pos_ref, bst_ref, thr_ref, dthr_ref, *, cap, pchunk):
    e_n = all_ref.shape[0]

    @pl.when(pl.program_id(0) == 0)
    def _():
        a3 = all_ref[...]

        def counts(x):
            return jnp.sum(jnp.sum(x, axis=2, keepdims=True), axis=1, keepdims=True)

        def largest(n_iter, hi0, admits):
            def step(_, carry):
                lo, hi = carry
                mid = lo + ((hi - lo + 1) >> 1)
                ok = admits(mid)
                return jnp.where(ok, mid, lo), jnp.where(ok, hi, mid - 1)
            init = (jnp.zeros((e_n, 1, 1), I32), jnp.full((e_n, 1, 1), hi0, I32))
            return lax.fori_loop(0, n_iter, step, init)[0]

        tbits = largest(31, 0x7F800000,
                        lambda m: counts((a3 >= lax.bitcast_convert_type(m, F32)).astype(F32)) >= cap)
        thr3 = lax.bitcast_convert_type(tbits, F32)
        d3 = a3 - thr3
        fine = lax.bitcast_convert_type(jnp.maximum((tbits >> 23) - 47, 1) << 23, F32)
        kfine = largest(24, (1 << 24) - 1, lambda m: counts((d3 >= m.astype(F32) * fine).astype(F32)) >= cap)
        thr_ref[...] = jnp.broadcast_to(thr3, thr_ref.shape)
        dthr_ref[...] = jnp.broadcast_to(kfine.astype(F32) * fine, dthr_ref.shape)

    a = aff_ref[0]
    nb = a.shape[0]
    ex = pl.program_id(0)
    d = a - thr_ref[ex][0:1, 0:1]
    dthr = dthr_ref[ex][0:1, 0:1]

    def total(x):
        return jnp.sum(jnp.sum(x, axis=0, keepdims=True), axis=1, keepdims=True)

    li = lax.broadcasted_iota(I32, (LANES, LANES), 0)
    lj = lax.broadcasted_iota(I32, (LANES, LANES), 1)
    upper = (li <= lj).astype(BF16)
    bi = lax.broadcasted_iota(I32, (nb, nb), 0)
    bj = lax.broadcasted_iota(I32, (nb, nb), 1)
    strict = (bj < bi).astype(BF16)
    incl = (bi <= bj).astype(BF16)
    ones8 = jnp.ones((8, LANES), BF16)

    def prefix(mask_b):
        rowcum = _dot(mask_b, upper)
        rowtot = jnp.broadcast_to(rowcum[:, LANES - 1:LANES], (nb, LANES)).astype(BF16)
        return rowcum + _dot(strict, rowtot)

    gt = d > dthr
    eq = d == dthr
    need = cap - total(gt.astype(F32))
    sel = jnp.logical_or(gt, jnp.logical_and(eq, prefix(eq.astype(BF16)) <= need))
    selb = sel.astype(BF16)
    cnt = prefix(selb)

    rt_row = _dot_nt(ones8, selb)
    bend = _dot(rt_row.astype(BF16), incl)
    bstart = bend - rt_row
    pos_ref[0] = jnp.where(sel, cnt - 1.0, -1.0).astype(I32)
    bst_ref[0] = bstart[0:1, :].astype(I32)

    cnt_hi = jnp.floor(cnt * (1.0 / 64.0))
    cnt_lo = cnt - 64.0 * cnt_hi
    a_hi, a_mid, a_lo = _split3(a)
    rowid = lax.broadcasted_iota(I32, (nb, LANES), 0).astype(BF16)
    rhs = jnp.concatenate([cnt_hi.astype(BF16), cnt_lo.astype(BF16), selb, rowid, a_hi, a_mid, a_lo], axis=1)
    lanef = lax.broadcasted_iota(I32, (pchunk, LANES), 1).astype(F32)

    for pc in range(cap // pchunk):
        p = (lax.broadcasted_iota(I32, (pchunk, nb), 0) + pc * pchunk).astype(F32)
        g = jnp.logical_and(bstart[0:1, :] <= p, p < bend[0:1, :]).astype(BF16)
        rows = _dot(g, rhs)
        part = lambda k: rows[:, k * LANES:(k + 1) * LANES]
        crow = part(0) * 64.0 + part(1)
        p1 = (lax.broadcasted_iota(I32, (pchunk, LANES), 0) + (pc * pchunk + 1)).astype(F32)
        oh = jnp.logical_and(crow == p1, part(2) > 0.5)
        pick = lambda x: jnp.where(oh, x, 0.0)
        tok_row = _dot_nt(ones8, pick(part(3)).astype(BF16))
        tok_lane = _dot_nt(ones8, pick(lanef).astype(BF16))
        idx_ref[0, :, pc * pchunk:(pc + 1) * pchunk] = (tok_row[0:1] * float(LANES) + tok_lane[0:1]).astype(I32)
        gate = jnp.sum(pick(part(4) + part(5) + part(6)), axis=1, keepdims=True)
        gate_ref[pc * pchunk:(pc + 1) * pchunk, :] = jnp.broadcast_to(gate, (pchunk, LANES))


def _route(aff_t, cap):
    e, n = aff_t.shape
    nb = n // LANES
    assert nb <= 256 and cap % 64 == 0 and cap // 64 <= 256
    pchunk = min(1024, cap)
    aff3 = aff_t.reshape(e, nb, LANES)
    return pl.pallas_call(
        functools.partial(_route_kernel, cap=cap, pchunk=pchunk),
        grid=(e,),
        in_specs=[_const_spec(aff3.shape), pl.BlockSpec((1, nb, LANES), lambda i: (i, 0, 0))],
        out_specs=(pl.BlockSpec((1, 1, cap), lambda i: (i, 0, 0)),
                   pl.BlockSpec((cap, LANES), lambda i: (i, 0)),
                   pl.BlockSpec((1, nb, LANES), lambda i: (i, 0, 0)),
                   pl.BlockSpec((1, 1, nb), lambda i: (i, 0, 0))),
        out_shape=(jax.ShapeDtypeStruct((e, 1, cap), I32), jax.ShapeDtypeStruct((e * cap, LANES), F32),
                   jax.ShapeDtypeStruct((e, nb, LANES), I32), jax.ShapeDtypeStruct((e, 1, nb), I32)),
        scratch_shapes=[pltpu.VMEM((e, SUBLANES, LANES), F32), pltpu.VMEM((e, SUBLANES, LANES), F32)],
        name="route",
        compiler_params=_params("arbitrary"),
    )(aff3, aff3)


def _ffn_kernel(idx_ref, nxt_ref, nx2_ref, gate_ref, wg_ref, wu_ref, wd_ref, h2_hbm, ye_ref, xbuf, sem):
    s = pl.program_id(0)
    slot = s % FFN_SLOTS
    ahead = (s + FFN_SLOTS - 1) % FFN_SLOTS

    def row_copy(rows_ref, dst_slot, i):
        return pltpu.make_async_copy(h2_hbm.at[rows_ref[0, 0, i]], xbuf.at[dst_slot, i], sem.at[dst_slot])

    def wait_slot(dst_slot):
        pltpu.make_async_copy(h2_hbm.at[pl.ds(0, FFN_ROWS)], xbuf.at[dst_slot], sem.at[dst_slot]).wait()

    @pl.when(s == 0)
    def _():
        def body(i, carry):
            row_copy(idx_ref, 0, i).start()
            row_copy(nxt_ref, 1, i).start()
            return carry
        lax.fori_loop(0, FFN_ROWS, body, 0, unroll=8)

    wait_slot(slot)
    x = jnp.concatenate([xbuf[slot, :, c, :] for c in range(D_MODEL // LANES)], axis=-1).astype(BF16)
    a = _dot(x, wg_ref[0])
    u = _dot(x, wu_ref[0])
    hmid = (a * jax.nn.sigmoid(a) * u).astype(BF16)
    y = _dot(hmid, wd_ref[0])
    for i in range(FFN_ROWS):
        row_copy(nx2_ref, ahead, i).start(priority=i % 2)
    ye_ref[...] = y * gate_ref[:, 0:1]

    @pl.when(s + 1 == pl.num_programs(0))
    def _():
        wait_slot((s + 1) % FFN_SLOTS)
        wait_slot(ahead)


def _ffn(idx, gate, wg, wu, wd, h2, cap):
    e = idx.shape[0]
    tr = FFN_ROWS
    per = cap // tr
    nsteps = e * per
    assert cap % tr == 0 and FFN_SLOTS == 3 and nsteps >= FFN_SLOTS
    wspec = lambda shp: pl.BlockSpec((1,) + shp, lambda s: (s // per, 0, 0))
    rows = lambda shift: pl.BlockSpec(
        (1, 1, tr), lambda s: (jnp.minimum(s + shift, nsteps - 1) // per, 0, jnp.minimum(s + shift, nsteps - 1) % per),
        memory_space=pltpu.SMEM)
    return pl.pallas_call(
        _ffn_kernel,
        grid=(nsteps,),
        in_specs=[rows(0), rows(1), rows(2), pl.BlockSpec((tr, LANES), lambda s: (s, 0)),
                  wspec((D_MODEL, D_FF)), wspec((D_MODEL, D_FF)), wspec((D_FF, D_MODEL)),
                  pl.BlockSpec(memory_space=pl.ANY)],
        out_specs=pl.BlockSpec((tr, D_MODEL), lambda s: (s, 0)),
        out_shape=jax.ShapeDtypeStruct((e * cap, D_MODEL), F32),
        scratch_shapes=[pltpu.VMEM((FFN_SLOTS, tr, D_MODEL // LANES, LANES), F32),
                        pltpu.SemaphoreType.DMA((FFN_SLOTS,))],
        name="ffn",
        compiler_params=_params("arbitrary"),
    )(idx, idx, idx, gate, wg, wu, wd, h2)


def _combine_kernel(bst_ref, pos_ref, h2_ref, g_ref, b_ref, ye_hbm, o_ref, ybuf, sem, *, cap):
    e_n = N_EXPERTS
    w = CMB_WIN
    step = w - SUBLANES
    per = CMB_TILE // LANES
    t = pl.program_id(0)
    last = pl.num_programs(0) - 1
    slot = t % 2
    wi = lax.broadcasted_iota(I32, (w, CMB_TILE), 0)

    def windows(tile, r):
        lo = [bst_ref[e, tile * per] + r * step for e in range(e_n)]
        src = [pl.multiple_of(jnp.minimum(e * cap + (lo[e] // SUBLANES) * SUBLANES, e_n * cap - w), SUBLANES)
               for e in range(e_n)]
        return lo, src

    def copies(src, dst_slot):
        return [pltpu.make_async_copy(ye_hbm.at[pl.ds(src[e], w)], ybuf.at[dst_slot, pl.ds(e * w, w)],
                                      sem.at[dst_slot, e]) for e in range(e_n)]

    def contribution(lo, src, buf_slot):
        onehot = []
        for e in range(e_n):
            base = src[e] - e * cap
            p = pos_ref[e:e + 1, :]
            this_round = jnp.logical_and(p >= lo[e], p < lo[e] + step)
            onehot.append(jnp.logical_and(p == base + wi, this_round).astype(BF16))
        oh_t = jnp.concatenate(onehot, axis=0)
        yv = ybuf[buf_slot]
        hi = yv.astype(BF16)
        lo_part = (yv - hi.astype(F32)).astype(BF16)
        return _dot_tn(oh_t, hi) + _dot_tn(oh_t, lo_part)

    lo0, src0 = windows(t, 0)

    @pl.when(t == 0)
    def _():
        for c in copies(src0, 0):
            c.start()

    _, src_next = windows(jnp.minimum(t + 1, last), 0)
    for c in copies(src_next, 1 - slot):
        c.start()
    for c in copies(src0, slot):
        c.wait()
    y = contribution(lo0, src0, slot)

    counts = [bst_ref[e, (t + 1) * per] - bst_ref[e, t * per] for e in range(e_n)]
    nrounds = functools.reduce(jnp.maximum, [(c + (step - 1)) // step for c in counts])

    def extra_round(r, y):
        lo, src = windows(t, r)
        for c in copies(src, slot):
            c.start()
        for c in copies(src, slot):
            c.wait()
        return y + contribution(lo, src, slot)

    y = lax.fori_loop(1, nrounds, extra_round, y)
    o_ref[...] = _layer_norm(DEEPNORM_ALPHA * h2_ref[...] + y, g_ref[...], b_ref[...])

    @pl.when(t == last)
    def _():
        for c in copies(src_next, 1 - slot):
            c.wait()


def _combine(bst, pos, h2, g, b, ye, cap):
    n = h2.shape[0]
    tt = CMB_TILE
    assert cap >= CMB_WIN and n % tt == 0
    grid_spec = pltpu.PrefetchScalarGridSpec(
        num_scalar_prefetch=1,
        grid=(n // tt,),
        in_specs=[pl.BlockSpec((N_EXPERTS, tt), lambda i, *_: (0, i)),
                  pl.BlockSpec((tt, D_MODEL), lambda i, *_: (i, 0)),
                  pl.BlockSpec(g.shape, lambda i, *_: (0, 0)), pl.BlockSpec(b.shape, lambda i, *_: (0, 0)),
                  pl.BlockSpec(memory_space=pl.ANY)],
        out_specs=pl.BlockSpec((tt, D_MODEL), lambda i, *_: (i, 0)),
        scratch_shapes=[pltpu.VMEM((2, N_EXPERTS * CMB_WIN, D_MODEL), F32),
                        pltpu.SemaphoreType.DMA((2, N_EXPERTS))],
    )
    return pl.pallas_call(
        functools.partial(_combine_kernel, cap=cap),
        grid_spec=grid_spec,
        out_shape=jax.ShapeDtypeStruct((n, D_MODEL), F32),
        name="combine_ln3",
        compiler_params=_params("arbitrary"),
    )(bst, pos, h2, g, b, ye)


def _prepare(ln_in_g, ln_in_b, w_in, na_rpb, gla_gate_w2, gla_gate_b, gla_norm_g, w_out, ln1_g, ln1_b,
             mem_wq, mem_wkv, mem_wo, ln2_g, ln2_b, w_router, w_gate, w_up, w_down, ln3_g, ln3_b):
    row = lambda v: v.reshape(1, -1).astype(F32)
    w = w_in[0]
    o = np.cumsum((0, NA_WIDTH, NA_WIDTH, NA_WIDTH, GLA_KEY_WIDTH, GLA_KEY_WIDTH, GLA_WIDTH, GLA_WIDTH,
                   2 * GLA_GATE_RANK))
    wgk = jnp.zeros((D_MODEL, LANES), F32).at[:, :2 * GLA_GATE_RANK].set(w[:, o[7]:o[8]])
    w2 = jnp.zeros((LANES, 2 * GLA_KEY_WIDTH), F32)
    for s in range(2):
        w2 = w2.at[s * GLA_GATE_RANK:(s + 1) * GLA_GATE_RANK,
                   s * GLA_KEY_WIDTH:(s + 1) * GLA_KEY_WIDTH].set(gla_gate_w2[0, s])
    tab, kaug = _na_tables(na_rpb[0])
    wr_t = w_router[0].T.astype(F32)
    wr_hi = wr_t.astype(BF16)
    wr_lo = (wr_t - wr_hi.astype(F32)).astype(BF16)
    return dict(
        ln_in=(row(ln_in_g), row(ln_in_b)),
        wna=w[:, o[0]:o[3]].astype(BF16), wqk=w[:, o[3]:o[5]].astype(BF16), wv=w[:, o[5]:o[6]].astype(BF16),
        wr=w[:, o[6]:o[7]].astype(BF16), wgk=wgk.astype(BF16), w2=w2.astype(BF16),
        gb=gla_gate_b[0].reshape(1, -1).astype(F32),
        tab=tab, kaug=kaug, ng=row(gla_norm_g[0]),
        wo1=w_out[0][:NA_WIDTH].astype(BF16), wo2=w_out[0][NA_WIDTH:].astype(BF16),
        ln1=(row(ln1_g[0]), row(ln1_b[0])),
        wq=mem_wq[0].astype(BF16), wkv=mem_wkv[0].astype(BF16), wmo=mem_wo[0].astype(BF16),
        ln2=(row(ln2_g[0]), row(ln2_b[0])),
        wr_hi=wr_hi, wr_lo=wr_lo,
        wg=w_gate[0].astype(BF16), wu=w_up[0].astype(BF16), wd=w_down[0].astype(BF16),
        ln3=(row(ln3_g[0]), row(ln3_b[0])),
    )


def _trunk(x, mem, p):
    batch, seq, _ = x.shape
    n = batch * seq
    cap = EC_CAPACITY_FACTOR * n // N_EXPERTS
    h0, qkv, qk, v, r, la = _in_proj(x.reshape(n, D_MODEL), *p["ln_in"], p["wna"], p["wqk"], p["wv"], p["wr"],
                                     p["wgk"], p["w2"], p["gb"])
    na = _na(qkv, p["tab"], p["kaug"], batch, seq)
    of, ob = _gla(qk, v, la, batch, seq)
    kv = _mem_kv(mem.reshape(batch * MEM_TOKENS, D_MODEL), p["wkv"])
    h2, h2t, aff_t = _xattn(na, of, ob, r, p["ng"], h0, p["wo1"], p["wo2"], *p["ln1"], kv, p["wq"], p["wmo"],
                            *p["ln2"], p["wr_hi"], p["wr_lo"], batch, seq)
    idx, gate, pos, bst = _route(aff_t, cap)
    ye = _ffn(idx, gate, p["wg"], p["wu"], p["wd"], h2t, cap)
    bst = jnp.concatenate([bst[:, 0, :], jnp.full((N_EXPERTS, 1), cap, I32)], axis=1)
    out = _combine(bst, pos.reshape(N_EXPERTS, n), h2, *p["ln3"], ye, cap)
    return out.reshape(batch, seq, D_MODEL)


def kernel(x_prompt, x_sample, mem_prompt, mem_sample, ln_in_g, ln_in_b, w_in, na_rpb, gla_gate_w2, gla_gate_b,
           gla_norm_g, w_out, ln1_g, ln1_b, mem_wq, mem_wkv, mem_wo, ln2_g, ln2_b, w_router, w_gate, w_up, w_down,
           ln3_g, ln3_b):
    p = _prepare(ln_in_g, ln_in_b, w_in, na_rpb, gla_gate_w2, gla_gate_b, gla_norm_g, w_out, ln1_g, ln1_b,
                 mem_wq, mem_wkv, mem_wo, ln2_g, ln2_b, w_router, w_gate, w_up, w_down, ln3_g, ln3_b)
    return _trunk(x_prompt, mem_prompt, p), _trunk(x_sample, mem_sample, p)
```
